```python
import jax
import jax.numpy as jnp
from jax import lax
import numpy as np

D_MODEL = 2048
BATCH = 4
SEQ = 4096
DEPTH = 1

EPS = 1e-6
GDN_HEADS = 8
GDN_DK = 128
GDN_DV = 128
CONV_WIDTH = 4
GDN_CHUNK = 64
RET_HEADS = 4
RET_DK = 256
RET_DV = 256
RET_CHUNK = 64
ROPE_BASE = 10000.0
N_EXPERTS = 32
TOP_K = 4
D_EXPERT = D_MODEL
SWIGLU_LIMIT = 7.0
SWIGLU_ALPHA = 1.702
MOE_BLOCK = 256

GDN_QK_W = GDN_HEADS * GDN_DK
GDN_V_W = GDN_HEADS * GDN_DV
RET_QK_W = RET_HEADS * RET_DK
RET_V_W = RET_HEADS * RET_DV
MIX_WIDTH = GDN_V_W + RET_V_W
GDN_CONV_CH = 2 * GDN_QK_W + GDN_V_W
IN_SIZES = (GDN_CONV_CH, GDN_V_W, GDN_HEADS, GDN_HEADS, RET_QK_W, RET_QK_W, RET_V_W, RET_V_W)
IN_COLS = GDN_CONV_CH + GDN_V_W + 2 * GDN_HEADS + 2 * RET_QK_W + 2 * RET_V_W

kernel_name = "hybrid_gdn_retention_moe_adaln_block"


def rmsnorm(x, g):
    x32 = x.astype(jnp.float32)
    y = x32 * lax.rsqrt(jnp.mean(x32 * x32, axis=-1, keepdims=True) + EPS)
    return (y * g.astype(jnp.float32)).astype(x.dtype)


def l2norm(x):
    return x * lax.rsqrt(jnp.sum(x * x, axis=-1, keepdims=True) + EPS)


def causal_depthwise_conv(u, w):
    return lax.conv_general_dilated(u, w[:, None, :].astype(u.dtype), window_strides=(1,),
                                    padding=[(w.shape[0] - 1, 0)],
                                    dimension_numbers=('NWC', 'WIO', 'NWC'),
                                    feature_group_count=u.shape[-1])


def apply_rotary(x, positions):
    half = x.shape[-1] // 2
    inv_freq = jnp.power(ROPE_BASE, -jnp.linspace(0.0, 1.0, half, dtype=jnp.float32))
    ang = positions.astype(jnp.float32)[..., None] * inv_freq
    cos = jnp.cos(ang)[:, :, None, :]
    sin = jnp.sin(ang)[:, :, None, :]
    x1, x2 = x[..., :half], x[..., half:]
    return jnp.concatenate([x1 * cos - x2 * sin, x2 * cos + x1 * sin], axis=-1)


def to_chunks(t, chunk):
    b, s, h = t.shape[:3]
    t = t.reshape((b, s // chunk, chunk, h) + t.shape[3:])
    perm = (0, 3, 1, 2) + tuple(range(4, t.ndim))
    return t.transpose(perm)


def from_chunks(t):
    n, b, h, c, d = t.shape
    return t.transpose(1, 0, 3, 2, 4).reshape(b, n * c, h, d)


def gated_delta_rule(q, k, v, beta, g_step):
    C = GDN_CHUNK
    dv = v.shape[-1]
    q, k, v = (to_chunks(t, C) for t in (q, k, v))
    beta = to_chunks(beta, C)
    g = jnp.cumsum(to_chunks(g_step, C), axis=-1)
    incl = jnp.tril(jnp.ones((C, C), dtype=bool))
    strict = jnp.tril(jnp.ones((C, C), dtype=bool), k=-1)
    diff = g[..., :, None] - g[..., None, :]
    decay_incl = jnp.exp(jnp.where(incl, diff, -jnp.inf))
    decay_strict = jnp.where(strict, decay_incl, 0.0)
    lower = beta[..., :, None] * jnp.einsum('bhnck,bhnmk->bhncm', k, k) * decay_strict
    a_mat = lower + jnp.eye(C, dtype=lower.dtype)
    rhs = jnp.concatenate([beta[..., None] * v, (beta * jnp.exp(g))[..., None] * k], axis=-1)
    sol = lax.linalg.triangular_solve(a_mat, rhs, left_side=True, lower=True, unit_diagonal=True)
    u, w = sol[..., :dv], sol[..., dv:]
    a_qk = jnp.einsum('bhnck,bhnmk->bhncm', q, k) * decay_incl
    q_dec = q * jnp.exp(g)[..., None]
    k_end = k * jnp.exp(g[..., -1:] - g)[..., None]
    chunk_decay = jnp.exp(g[..., -1])
    xs = tuple(jnp.moveaxis(t, 2, 0) for t in (q_dec, k_end, u, w, a_qk, chunk_decay))

    def step(state, inp):
        qd, ke, uc, wc, aqk, cd = inp
        delta = uc - jnp.einsum('bhck,bhkv->bhcv', wc, state)
        out = jnp.einsum('bhck,bhkv->bhcv', qd, state) + jnp.einsum('bhcm,bhmv->bhcv', aqk, delta)
        state = cd[..., None, None] * state + jnp.einsum('bhck,bhcv->bhkv', ke, delta)
        return state, out

    b, h, _, _, dk = q.shape
    s0 = jnp.zeros((b, h, dk, dv), jnp.float32)
    _, out = lax.scan(step, s0, xs)
    return from_chunks(out)


def multiscale_retention(q, k, v):
    C = RET_CHUNK
    h = q.shape[2]
    log_gamma = jnp.log1p(-jnp.exp2(-5.0 - jnp.arange(h, dtype=jnp.float32)))
    q, k, v = (to_chunks(t, C) for t in (q, k, v))
    pos = jnp.arange(C, dtype=jnp.float32)
    incl = jnp.tril(jnp.ones((C, C), dtype=bool))
    d_mat = jnp.exp(jnp.where(incl, (pos[:, None] - pos[None, :]) * log_gamma[:, None, None], -jnp.inf))
    scores = jnp.einsum('bhnck,bhnmk->bhncm', q, k) * d_mat[None, :, None]
    inner = jnp.einsum('bhncm,bhnmv->bhncv', scores, v)
    q_dec = q * jnp.exp((pos + 1.0) * log_gamma[:, None])[None, :, None, :, None]
    k_dec = k * jnp.exp((C - 1.0 - pos) * log_gamma[:, None])[None, :, None, :, None]
    chunk_decay = jnp.exp(C * log_gamma)[None, :, None, None]
    xs = tuple(jnp.moveaxis(t, 2, 0) for t in (q_dec, k_dec, v, inner))

    def step(state, inp):
        qd, kd, vc, inn = inp
        out = inn + jnp.einsum('bhck,bhkv->bhcv', qd, state)
        state = chunk_decay * state + jnp.einsum('bhck,bhcv->bhkv', kd, vc)
        return state, out

    b, _, _, _, dk = q.shape
    s0 = jnp.zeros((b, h, dk, v.shape[-1]), jnp.float32)
    _, out = lax.scan(step, s0, xs)
    return from_chunks(out)


def hybrid_mixer(h, positions, w_in, conv_w, a_log, dt_bias, gdn_norm_g, ret_norm_g, ret_norm_b, w_out):
    b, s, _ = h.shape
    f32 = jnp.float32
    proj = h @ w_in
    cuts = np.cumsum(IN_SIZES)[:-1].tolist()
    g_qkv, g_z, g_b, g_a, r_q, r_k, r_v, r_g = jnp.split(proj, cuts, axis=-1)
    g_qkv = jax.nn.silu(causal_depthwise_conv(g_qkv, conv_w)).astype(f32)
    gq, gk, gv = jnp.split(g_qkv, [GDN_QK_W, 2 * GDN_QK_W], axis=-1)
    gq = l2norm(gq.reshape(b, s, GDN_HEADS, GDN_DK)) * (GDN_DK ** -0.5)
    gk = l2norm(gk.reshape(b, s, GDN_HEADS, GDN_DK))
    gv = gv.reshape(b, s, GDN_HEADS, GDN_DV)
    beta = jax.nn.sigmoid(g_b.astype(f32))
    g_step = -jnp.exp(a_log.astype(f32)) * jax.nn.softplus(g_a.astype(f32) + dt_bias.astype(f32))
    o_g = gated_delta_rule(gq, gk, gv, beta, g_step)
    o_g = o_g * lax.rsqrt(jnp.mean(o_g * o_g, axis=-1, keepdims=True) + EPS) * gdn_norm_g.astype(f32)
    o_g = (o_g * jax.nn.silu(g_z.astype(f32).reshape(b, s, GDN_HEADS, GDN_DV))).reshape(b, s, GDN_V_W)
    rq = apply_rotary(r_q.astype(f32).reshape(b, s, RET_HEADS, RET_DK), positions)
    rk = apply_rotary(r_k.astype(f32).reshape(b, s, RET_HEADS, RET_DK), positions) * (RET_DK ** -0.5)
    rv = r_v.astype(f32).reshape(b, s, RET_HEADS, RET_DV)
    o_r = multiscale_retention(rq, rk, rv)
    mu = jnp.mean(o_r, axis=-1, keepdims=True)
    var = jnp.mean(jnp.square(o_r - mu), axis=-1, keepdims=True)
    o_r = ((o_r - mu) * lax.rsqrt(var + EPS)).reshape(b, s, RET_V_W)
    o_r = (o_r * ret_norm_g.astype(f32) + ret_norm_b.astype(f32)) * jax.nn.silu(r_g.astype(f32))
    mixed = jnp.concatenate([o_g, o_r], axis=-1).astype(h.dtype)
    return mixed @ w_out


def moe_ffn(h, w_router, b_router, w1, b1, w2, b2):
    b, s, d = h.shape
    t = b * s
    xf = h.reshape(t, d)
    logits = (xf @ w_router + b_router).astype(jnp.float32)
    top_val, top_idx = lax.top_k(logits, TOP_K)
    gates = jax.nn.softmax(top_val, axis=-1)
    flat_e = top_idx.reshape(-1)
    flat_t = jnp.repeat(jnp.arange(t, dtype=jnp.int32), TOP_K)
    flat_g = gates.reshape(-1)
    order = jnp.argsort(flat_e)
    se, st, sg = flat_e[order], flat_t[order], flat_g[order]
    counts = jnp.bincount(flat_e, length=N_EXPERTS)
    starts = jnp.cumsum(counts) - counts
    padded = (counts + MOE_BLOCK - 1) // MOE_BLOCK * MOE_BLOCK
    pad_ends = jnp.cumsum(padded)
    pad_starts = pad_ends - padded
    dest = pad_starts[se] + jnp.arange(t * TOP_K, dtype=jnp.int32) - starts[se]
    n_blocks = (t * TOP_K) // MOE_BLOCK + N_EXPERTS
    n_rows = n_blocks * MOE_BLOCK
    row_tok = jnp.full((n_rows,), t, jnp.int32).at[dest].set(st)
    row_gate = jnp.zeros((n_rows,), jnp.float32).at[dest].set(sg)
    block_e = jnp.minimum(jnp.searchsorted(pad_ends, jnp.arange(n_blocks) * MOE_BLOCK, side='right'),
                          N_EXPERTS - 1)
    x_pad = jnp.concatenate([xf, jnp.zeros((1, d), xf.dtype)], axis=0)

    def expert_block(args):
        tok, gate, e = args
        hid = x_pad[tok] @ w1[e] + b1[e]
        x_glu, x_lin = jnp.split(hid, 2, axis=-1)
        x_glu = jnp.minimum(x_glu, SWIGLU_LIMIT)
        x_lin = jnp.clip(x_lin, -SWIGLU_LIMIT, SWIGLU_LIMIT)
        act = x_glu * jax.nn.sigmoid(SWIGLU_ALPHA * x_glu) * (x_lin + 1.0)
        return (act @ w2[e] + b2[e]) * gate[:, None].astype(hid.dtype)

    ys = lax.map(expert_block, (row_tok.reshape(n_blocks, MOE_BLOCK),
                                row_gate.reshape(n_blocks, MOE_BLOCK), block_e))
    out = jax.ops.segment_sum(ys.reshape(n_rows, d), row_tok, num_segments=t + 1)[:t]
    return out.reshape(b, s, d)


def setup_inputs(seed: int = 0) -> dict:
    key = jax.random.key(seed)
    ks = jax.random.split(key, 24)
    f32 = jnp.float32
    L, D, E, F = DEPTH, D_MODEL, N_EXPERTS, D_EXPERT

    def nrm(k, shape, scale):
        return scale * jax.random.normal(k, shape, f32)

    x = jax.random.normal(ks[0], (BATCH, SEQ, D), f32)
    c = jax.random.normal(ks[1], (BATCH, D), f32)
    start = jax.random.randint(ks[2], (BATCH, 1), 0, 1024, dtype=jnp.int32)
    positions = start + jnp.arange(SEQ, dtype=jnp.int32)[None, :]
    return {
        'x': x,
        'c': c,
        'positions': positions,
        'ada_w': nrm(ks[3], (L, D, 6 * D), D ** -0.5),
        'ada_b': nrm(ks[4], (L, 6 * D), 0.02),
        'norm1_g': 1.0 + nrm(ks[5], (L, D), 0.02),
        'w_in': nrm(ks[6], (L, D, IN_COLS), D ** -0.5),
        'conv_w': nrm(ks[7], (L, CONV_WIDTH, GDN_CONV_CH), CONV_WIDTH ** -0.5),
        'gdn_a_log': jnp.log(jax.random.uniform(ks[8], (L, GDN_HEADS), f32, 1.0, 16.0)),
        'gdn_dt_bias': nrm(ks[9], (L, GDN_HEADS), 0.5),
        'gdn_norm_g': 1.0 + nrm(ks[10], (L, GDN_DV), 0.02),
        'ret_norm_g': 1.0 + nrm(ks[11], (L, RET_V_W), 0.02),
        'ret_norm_b': nrm(ks[12], (L, RET_V_W), 0.02),
        'w_out': nrm(ks[13], (L, MIX_WIDTH, D), MIX_WIDTH ** -0.5),
        'norm2_g': 1.0 + nrm(ks[14], (L, D), 0.02),
        'w_router': nrm(ks[15], (L, D, E), D ** -0.5),
        'b_router': nrm(ks[16], (L, E), 0.01),
        'w1': nrm(ks[17], (L, E, D, 2 * F), D ** -0.5),
        'b1': nrm(ks[18], (L, E, 2 * F), 0.01),
        'w2': nrm(ks[19], (L, E, F, D), F ** -0.5),
        'b2': nrm(ks[20], (L, E, D), 0.01),
        'final_norm_g': 1.0 + nrm(ks[21], (D,), 0.02),
    }


def reference(x, c, positions, ada_w, ada_b, norm1_g, w_in, conv_w, gdn_a_log, gdn_dt_bias, gdn_norm_g,
              ret_norm_g, ret_norm_b, w_out, norm2_g, w_router, b_router, w1, b1, w2, b2, final_norm_g):
    cond = jax.nn.silu(c)
    for l in range(DEPTH):
        mod = (cond @ ada_w[l] + ada_b[l])[:, None, :]
        sh1, sc1, gt1, sh2, sc2, gt2 = jnp.split(mod, 6, axis=-1)
        h = rmsnorm(x, norm1_g[l]) * (1.0 + sc1) + sh1
        x = x + gt1 * hybrid_mixer(h, positions, w_in[l], conv_w[l], gdn_a_log[l], gdn_dt_bias[l],
                                   gdn_norm_g[l], ret_norm_g[l], ret_norm_b[l], w_out[l])
        h = rmsnorm(x, norm2_g[l]) * (1.0 + sc2) + sh2
        x = x + gt2 * moe_ffn(h, w_router[l], b_router[l], w1[l], b1[l], w2[l], b2[l])
    return rmsnorm(x, final_norm_g)
```

```python
import functools
import math

import jax
import jax.numpy as jnp
from jax import lax
from jax.experimental import pallas as pl
from jax.experimental.pallas import tpu as pltpu

F32 = jnp.float32
BF16 = jnp.bfloat16
HIGHEST = lax.Precision.HIGHEST

EPS = 1e-6
GDN_HEADS = 8
GDN_DK = 128
GDN_DV = 128
CONV_WIDTH = 4
GDN_CHUNK = 64
RET_HEADS = 4
RET_DK = 256
RET_DV = 256
RET_CHUNK = 256
ROPE_BASE = 10000.0
N_EXPERTS = 32
TOP_K = 4
SWIGLU_LIMIT = 7.0
SWIGLU_ALPHA = 1.702

GDN_QK_W = GDN_HEADS * GDN_DK
GDN_V_W = GDN_HEADS * GDN_DV
GDN_CONV_CH = 2 * GDN_QK_W + GDN_V_W
RET_W = RET_HEADS * RET_DK

V7X_VMEM_BYTES = 64 * 1024 * 1024
VMEM_LIMIT = 56 * 1024 * 1024

MOE_TM = 1024
MOE_SUB = 256
MOE_TF = 512


def _dot(a, b, precision=None):
    return jnp.dot(a, b, preferred_element_type=F32, precision=precision)


def _dot_nt(a, b, precision=None):
    return lax.dot_general(a, b, (((1,), (1,)), ((), ())), preferred_element_type=F32, precision=precision)


def _dot_tn(a, b, precision=None):
    return lax.dot_general(a, b, (((0,), (0,)), ((), ())), preferred_element_type=F32, precision=precision)


def _sigmoid(x):
    return 1.0 / (1.0 + jnp.exp(-x))


def _silu(x):
    return x * _sigmoid(x)


def _cparams(sem, vmem=VMEM_LIMIT):
    return pltpu.CompilerParams(dimension_semantics=sem, vmem_limit_bytes=vmem)


def _ada_kernel(c_ref, w_ref, b_ref, o_ref):
    cond = _silu(c_ref[...])
    o_ref[...] = _dot(cond, w_ref[...], precision=HIGHEST) + b_ref[...]


def _ada(c8, ada_w, ada_b):
    d, n = ada_w.shape
    tn = 1024
    return pl.pallas_call(
        _ada_kernel,
        grid=(n // tn,),
        in_specs=[pl.BlockSpec((8, d), lambda j: (0, 0)),
                  pl.BlockSpec((d, tn), lambda j: (0, j)),
                  pl.BlockSpec((1, tn), lambda j: (0, j))],
        out_specs=pl.BlockSpec((8, tn), lambda j: (0, j)),
        out_shape=jax.ShapeDtypeStruct((8, n), F32),
        compiler_params=_cparams(("arbitrary",)),
        name="ada_mod",
    )(c8, ada_w, ada_b.reshape(1, n))


def _inproj_kernel(x_ref, g_ref, sc_ref, sh_ref, w_ref, wba_ref, proj_ref, ba_ref, h_ref):
    @pl.when(pl.program_id(1) == 0)
    def _():
        x = x_ref[...]
        y = x * lax.rsqrt(jnp.mean(x * x, axis=-1, keepdims=True) + EPS)
        h = (y * g_ref[...]) * (1.0 + sc_ref[0]) + sh_ref[0]
        hb = h.astype(BF16)
        h_ref[...] = hb
        ba_ref[...] = _dot(hb, wba_ref[...])

    proj_ref[...] = _dot(h_ref[...], w_ref[...]).astype(BF16)


def _inproj(xf, g1, sc1, sh1, w_main, w_ba, seq):
    t, d = xf.shape
    n = w_main.shape[1]
    tm, tn = 1024, 1024
    per_b = seq // tm
    return pl.pallas_call(
        _inproj_kernel,
        grid=(t // tm, n // tn),
        in_specs=[pl.BlockSpec((tm, d), lambda i, j: (i, 0)),
                  pl.BlockSpec((1, d), lambda i, j: (0, 0)),
                  pl.BlockSpec((1, 1, d), lambda i, j: (i // per_b, 0, 0)),
                  pl.BlockSpec((1, 1, d), lambda i, j: (i // per_b, 0, 0)),
                  pl.BlockSpec((d, tn), lambda i, j: (0, j)),
                  pl.BlockSpec((d, 256), lambda i, j: (0, 0))],
        out_specs=[pl.BlockSpec((tm, tn), lambda i, j: (i, j)),
                   pl.BlockSpec((tm, 256), lambda i, j: (i, 0))],
        out_shape=[jax.ShapeDtypeStruct((t, n), BF16), jax.ShapeDtypeStruct((t, 256), F32)],
        scratch_shapes=[pltpu.VMEM((tm, d), BF16)],
        compiler_params=_cparams(("arbitrary", "arbitrary")),
        name="norm1_inproj",
    )(xf, g1, sc1, sh1, w_main, w_ba)


def _unit_lower_inverse(lm, n):
    r = lax.broadcasted_iota(jnp.int32, (n, n), 0)
    c = lax.broadcasted_iota(jnp.int32, (n, n), 1)
    eye = (r == c).astype(F32)

    def mm(a, b):
        return _dot(a.astype(BF16), b.astype(BF16))

    dblk = jnp.where((r // 8) == (c // 8), lm, 0.0)
    d2 = mm(dblk, dblk)
    x = eye - dblk
    x = x + mm(x, d2)
    d4 = mm(d2, d2)
    x = x + mm(x, d4)
    s = 8
    while s < n:
        off = jnp.where(((r // (2 * s)) == (c // (2 * s))) & ((r // s) != (c // s)), lm, 0.0)
        x = x - mm(mm(x, off), x)
        s *= 2
    return x


def _gdn_kernel(qkvz_ref, ba_ref, convw_ref, alog_ref, dtb_ref, ng_ref, o_ref, hist_ref, stage_ref, s_ref):
    C = GDN_CHUNK

    @pl.when(pl.program_id(1) == 0)
    def _():
        hist_ref[...] = jnp.zeros_like(hist_ref)
        s_ref[...] = jnp.zeros_like(s_ref)

    u_in = qkvz_ref[:, :GDN_CONV_CH].astype(F32)
    stage_ref[0:8, :] = hist_ref[...]
    stage_ref[8:8 + C, :] = u_in
    hist_ref[...] = u_in[C - 8:, :]

    beta_all = _sigmoid(ba_ref[:, :128])
    a_in = ba_ref[:, 128:] + dtb_ref[...]
    softplus = jnp.maximum(a_in, 0.0) + jnp.log(1.0 + jnp.exp(-jnp.abs(a_in)))
    gstep = -jnp.exp(alog_ref[...]) * softplus
    r = lax.broadcasted_iota(jnp.int32, (C, C), 0)
    c = lax.broadcasted_iota(jnp.int32, (C, C), 1)
    incl = r >= c
    strict = r > c
    gcum = _dot(incl.astype(F32), gstep, precision=HIGHEST)
    gcum_t = gcum.T

    def conv(col0):
        acc = None
        for i in range(CONV_WIDTH):
            term = stage_ref[5 + i:5 + i + C, col0:col0 + 128] * convw_ref[i:i + 1, col0:col0 + 128]
            acc = term if acc is None else acc + term
        return _silu(acc)

    for h in range(GDN_HEADS):
        qc = conv(h * GDN_DK)
        kc = conv(GDN_QK_W + h * GDN_DK)
        vc = conv(2 * GDN_QK_W + h * GDN_DV)
        qn = qc * lax.rsqrt(jnp.sum(qc * qc, axis=-1, keepdims=True) + EPS) * (GDN_DK ** -0.5)
        kn = kc * lax.rsqrt(jnp.sum(kc * kc, axis=-1, keepdims=True) + EPS)
        beta = beta_all[:, h:h + 1]
        g = gcum[:, h:h + 1]
        grow = gcum_t[h:h + 1, :]
        glast = gcum[C - 1:C, h:h + 1]
        dec_incl = jnp.exp(jnp.where(incl, g - grow, -1e30))
        dec_strict = jnp.where(strict, dec_incl, 0.0)
        kb = kn.astype(BF16)
        lm = beta * _dot_nt(kb, kb) * dec_strict
        ainv = _unit_lower_inverse(lm, C)
        eg = jnp.exp(g)
        rhs = jnp.concatenate([beta * vc, (beta * eg) * kn], axis=1)
        sol = _dot(ainv.astype(BF16), rhs.astype(BF16))
        u = sol[:, :GDN_DV]
        w = sol[:, GDN_DV:]
        aqk = _dot_nt(qn.astype(BF16), kb) * dec_incl
        q_dec = qn * eg
        k_end = kn * jnp.exp(glast - g)
        state = s_ref[h]
        sb = state.astype(BF16)
        delta = u - _dot(w.astype(BF16), sb)
        db = delta.astype(BF16)
        out = _dot(q_dec.astype(BF16), sb) + _dot(aqk.astype(BF16), db)
        s_ref[h] = jnp.exp(glast) * state + _dot_tn(k_end.astype(BF16), db)
        z = qkvz_ref[:, GDN_CONV_CH + h * GDN_DV:GDN_CONV_CH + (h + 1) * GDN_DV].astype(F32)
        o = out * lax.rsqrt(jnp.mean(out * out, axis=-1, keepdims=True) + EPS) * ng_ref[...] * _silu(z)
        o_ref[:, h * GDN_DV:(h + 1) * GDN_DV] = o.astype(BF16)


def _gdn(proj, ba, conv_w, a_log128, dtb128, norm_g, batch, seq):
    t = proj.shape[0]
    C = GDN_CHUNK
    nc = seq // C
    return pl.pallas_call(
        _gdn_kernel,
        grid=(batch, nc),
        in_specs=[pl.BlockSpec((C, 4096), lambda b, s: (b * nc + s, 0)),
                  pl.BlockSpec((C, 256), lambda b, s: (b * nc + s, 0)),
                  pl.BlockSpec((CONV_WIDTH, GDN_CONV_CH), lambda b, s: (0, 0)),
                  pl.BlockSpec((1, 128), lambda b, s: (0, 0)),
                  pl.BlockSpec((1, 128), lambda b, s: (0, 0)),
                  pl.BlockSpec((1, 128), lambda b, s: (0, 0))],
        out_specs=pl.BlockSpec((C, GDN_V_W), lambda b, s: (b * nc + s, 0)),
        out_shape=jax.ShapeDtypeStruct((t, GDN_V_W), BF16),
        scratch_shapes=[pltpu.VMEM((8, GDN_CONV_CH), F32),
                        pltpu.VMEM((C + 8, GDN_CONV_CH), F32),
                        pltpu.VMEM((GDN_HEADS, GDN_DK, GDN_DV), F32)],
        compiler_params=_cparams(("arbitrary", "arbitrary")),
        name="gdn_heads",
    )(proj, ba, conv_w, a_log128, dtb128, norm_g)


def _ret_kernel(p_ref, pos_ref, invf_ref, ng_ref, nb_ref, o_ref, s_ref):
    C = RET_CHUNK

    @pl.when(pl.program_id(1) == 0)
    def _():
        s_ref[...] = jnp.zeros_like(s_ref)

    ang = pos_ref[...] * invf_ref[...]
    cs = jnp.cos(ang)
    sn = jnp.sin(ang)
    r = lax.broadcasted_iota(jnp.int32, (C, C), 0)
    c = lax.broadcasted_iota(jnp.int32, (C, C), 1)
    incl = r >= c
    dpos = (r - c).astype(F32)
    pcol = lax.broadcasted_iota(jnp.int32, (C, 1), 0).astype(F32)
    half = RET_DK // 2

    for h in range(RET_HEADS):
        lg = math.log1p(-(2.0 ** (-5.0 - h)))
        q0 = h * RET_DK
        k0 = RET_W + h * RET_DK
        v0 = 2 * RET_W + h * RET_DV
        g0 = 3 * RET_W + h * RET_DV
        q1 = p_ref[:, q0:q0 + half].astype(F32)
        q2 = p_ref[:, q0 + half:q0 + RET_DK].astype(F32)
        k1 = p_ref[:, k0:k0 + half].astype(F32)
        k2 = p_ref[:, k0 + half:k0 + RET_DK].astype(F32)
        q = jnp.concatenate([q1 * cs - q2 * sn, q2 * cs + q1 * sn], axis=1)
        k = jnp.concatenate([k1 * cs - k2 * sn, k2 * cs + k1 * sn], axis=1) * (RET_DK ** -0.5)
        v = p_ref[:, v0:v0 + RET_DV]
        dmat = jnp.exp(jnp.where(incl, dpos * lg, -1e30))
        scores = _dot_nt(q.astype(BF16), k.astype(BF16)) * dmat
        inner = _dot(scores.astype(BF16), v)
        q_dec = q * jnp.exp((pcol + 1.0) * lg)
        k_dec = k * jnp.exp((C - 1.0 - pcol) * lg)
        state = s_ref[h]
        out = inner + _dot(q_dec.astype(BF16), state.astype(BF16))
        s_ref[h] = math.exp(C * lg) * state + _dot_tn(k_dec.astype(BF16), v)
        mu = jnp.mean(out, axis=-1, keepdims=True)
        cen = out - mu
        var = jnp.mean(cen * cen, axis=-1, keepdims=True)
        on = cen * lax.rsqrt(var + EPS)
        gate = p_ref[:, g0:g0 + RET_DV].astype(F32)
        o = (on * ng_ref[:, h * RET_DV:(h + 1) * RET_DV] + nb_ref[:, h * RET_DV:(h + 1) * RET_DV]) * _silu(gate)
        o_ref[:, h * RET_DV:(h + 1) * RET_DV] = o.astype(BF16)


def _ret(proj, posb, inv_freq, norm_g, norm_b, batch, seq):
    t = proj.shape[0]
    C = RET_CHUNK
    nc = seq // C
    return pl.pallas_call(
        _ret_kernel,
        grid=(batch, nc),
        in_specs=[pl.BlockSpec((C, 4096), lambda b, s: (b * nc + s, 1)),
                  pl.BlockSpec((C, 128), lambda b, s: (b * nc + s, 0)),
                  pl.BlockSpec((1, 128), lambda b, s: (0, 0)),
                  pl.BlockSpec((1, RET_W), lambda b, s: (0, 0)),
                  pl.BlockSpec((1, RET_W), lambda b, s: (0, 0))],
        out_specs=pl.BlockSpec((C, RET_W), lambda b, s: (b * nc + s, 0)),
        out_shape=jax.ShapeDtypeStruct((t, RET_W), BF16),
        scratch_shapes=[pltpu.VMEM((RET_HEADS, RET_DK, RET_DV), F32)],
        compiler_params=_cparams(("arbitrary", "arbitrary")),
        name="ret_heads",
    )(proj, posb, inv_freq, norm_g, norm_b)


def _outproj_kernel(og_ref, or_ref, wt_ref, wb_ref, x_ref, gt_ref, g2_ref, sc_ref, sh_ref, wr_ref, br_ref,
                    x1_ref, h2_ref, idx_ref, gate_ref):
    mix = _dot(og_ref[...], wt_ref[...]) + _dot(or_ref[...], wb_ref[...])
    x1 = x_ref[...] + gt_ref[0] * mix
    x1_ref[...] = x1
    y = x1 * lax.rsqrt(jnp.mean(x1 * x1, axis=-1, keepdims=True) + EPS)
    h2 = (y * g2_ref[...]) * (1.0 + sc_ref[0]) + sh_ref[0]
    h2_ref[...] = h2.astype(BF16)
    logits = _dot_nt(wr_ref[...], h2, precision=HIGHEST) + br_ref[...]
    eidx = lax.broadcasted_iota(jnp.int32, logits.shape, 0)
    vals = logits
    tops, idxs = [], []
    for _ in range(TOP_K):
        m = jnp.max(vals, axis=0, keepdims=True)
        sel = jnp.min(jnp.where(vals == m, eidx, N_EXPERTS), axis=0, keepdims=True)
        tops.append(m)
        idxs.append(sel)
        vals = jnp.where(eidx == sel, -jnp.inf, vals)
    exps = [jnp.exp(v - tops[0]) for v in tops]
    denom = exps[0] + exps[1] + exps[2] + exps[3]
    idx_ref[...] = jnp.concatenate(idxs, axis=0)
    gate_ref[...] = jnp.concatenate(exps, axis=0) / denom


def _outproj(o_g, o_r, w_top, w_bot, xf, gt1, g2, sc2, sh2, wr_t, br, seq):
    t, d = xf.shape
    tm = 512
    per_b = seq // tm
    row = lambda i: (i, 0)
    full = lambda i: (0, 0)
    perb = lambda i: (i // per_b, 0, 0)
    return pl.pallas_call(
        _outproj_kernel,
        grid=(t // tm,),
        in_specs=[pl.BlockSpec((tm, GDN_V_W), row),
                  pl.BlockSpec((tm, RET_W), row),
                  pl.BlockSpec((GDN_V_W, d), full),
                  pl.BlockSpec((RET_W, d), full),
                  pl.BlockSpec((tm, d), row),
                  pl.BlockSpec((1, 1, d), perb),
                  pl.BlockSpec((1, d), full),
                  pl.BlockSpec((1, 1, d), perb),
                  pl.BlockSpec((1, 1, d), perb),
                  pl.BlockSpec((N_EXPERTS, d), full),
                  pl.BlockSpec((N_EXPERTS, 1), full)],
        out_specs=[pl.BlockSpec((tm, d), row),
                   pl.BlockSpec((tm, d), row),
                   pl.BlockSpec((TOP_K, tm), lambda i: (0, i)),
                   pl.BlockSpec((TOP_K, tm), lambda i: (0, i))],
        out_shape=[jax.ShapeDtypeStruct((t, d), F32), jax.ShapeDtypeStruct((t, d), BF16),
                   jax.ShapeDtypeStruct((TOP_K, t), jnp.int32), jax.ShapeDtypeStruct((TOP_K, t), F32)],
        compiler_params=_cparams(("arbitrary",)),
        name="outproj_norm2_router",
    )(o_g, o_r, w_top, w_bot, xf, gt1, g2, sc2, sh2, wr_t, br)


def _moe_kernel(be_ref, nv_ref, nu_ref, x_ref, w1g_ref, w1l_ref, b1g_ref, b1l_ref, w2_ref, b2_ref, gate_ref,
                o_ref, act_ref, wgb_ref, wlb_ref, w2b_ref):
    i = pl.program_id(0)
    j = pl.program_id(1)
    nj = act_ref.shape[0]
    tm = x_ref.shape[0]
    nv = nv_ref[i]

    def up(rows):
        xs = x_ref[rows, :]
        hg = _dot(xs, wgb_ref[...]) + b1g_ref[0]
        hl = _dot(xs, wlb_ref[...]) + b1l_ref[0]
        hg = jnp.minimum(hg, SWIGLU_LIMIT)
        hl = jnp.clip(hl, -SWIGLU_LIMIT, SWIGLU_LIMIT)
        return (hg * _sigmoid(SWIGLU_ALPHA * hg) * (hl + 1.0)).astype(BF16)

    def down(rows):
        y = b2_ref[0] + _dot(act_ref[0, rows, :], w2b_ref[0:MOE_TF, :])
        for jj in range(1, nj):
            y = y + _dot(act_ref[jj, rows, :], w2b_ref[jj * MOE_TF:(jj + 1) * MOE_TF, :])
        return (y * gate_ref[rows, :]).astype(BF16)

    @pl.when(jnp.logical_and(i < nu_ref[0], j < nj))
    def _():
        wgb_ref[...] = w1g_ref[0].astype(BF16)
        wlb_ref[...] = w1l_ref[0].astype(BF16)

        @pl.when(nv == tm)
        def _():
            act_ref[j] = up(slice(None))

        @pl.when(nv < tm)
        def _():
            for s in range(tm // MOE_SUB):
                rows = slice(s * MOE_SUB, (s + 1) * MOE_SUB)

                @pl.when(s * MOE_SUB < nv)
                def _():
                    act_ref[j, rows, :] = up(rows)

    @pl.when(jnp.logical_and(i < nu_ref[0], j >= nj))
    def _():
        w2b_ref[...] = w2_ref[0].astype(BF16)

        @pl.when(nv == tm)
        def _():
            o_ref[...] = down(slice(None))

        @pl.when(nv < tm)
        def _():
            for s in range(tm // MOE_SUB):
                rows = slice(s * MOE_SUB, (s + 1) * MOE_SUB)

                @pl.when(s * MOE_SUB < nv)
                def _():
                    o_ref[rows, :] = down(rows)

                @pl.when(s * MOE_SUB >= nv)
                def _():
                    o_ref[rows, :] = jnp.zeros((MOE_SUB, o_ref.shape[1]), BF16)


def _moe(block_e, nvalid, nused, x_sorted, w1, b1, w2, b2, row_gate):
    r, d = x_sorted.shape
    e, _, f2 = w1.shape
    f = f2 // 2
    tm, tf = MOE_TM, MOE_TF
    nblk = r // tm
    nj = f // tf
    assert d // tf == nj

    def live(i, nu):
        return jnp.minimum(i, nu[0] - 1)

    def c1(i, j, nu):
        return jnp.where(i < nu[0], jnp.minimum(j, nj - 1), nj - 1)

    def c2(i, j, nu):
        return jnp.where(i < nu[0], jnp.maximum(j - nj, 0), nj - 1)

    grid_spec = pltpu.PrefetchScalarGridSpec(
        num_scalar_prefetch=3,
        grid=(nblk, 2 * nj),
        in_specs=[
            pl.BlockSpec((tm, d), lambda i, j, be, nv, nu: (live(i, nu), 0)),
            pl.BlockSpec((1, d, tf), lambda i, j, be, nv, nu: (be[live(i, nu)], 0, c1(i, j, nu))),
            pl.BlockSpec((1, d, tf), lambda i, j, be, nv, nu: (be[live(i, nu)], 0, nj + c1(i, j, nu))),
            pl.BlockSpec((1, 1, tf), lambda i, j, be, nv, nu: (be[live(i, nu)], 0, c1(i, j, nu))),
            pl.BlockSpec((1, 1, tf), lambda i, j, be, nv, nu: (be[live(i, nu)], 0, nj + c1(i, j, nu))),
            pl.BlockSpec((1, f, tf), lambda i, j, be, nv, nu: (be[live(i, nu)], 0, c2(i, j, nu))),
            pl.BlockSpec((1, 1, tf), lambda i, j, be, nv, nu: (be[live(i, nu)], 0, c2(i, j, nu))),
            pl.BlockSpec((tm, 1), lambda i, j, be, nv, nu: (live(i, nu), 0)),
        ],
        out_specs=pl.BlockSpec((tm, tf), lambda i, j, be, nv, nu: (live(i, nu), c2(i, j, nu))),
        scratch_shapes=[pltpu.VMEM((nj, tm, tf), BF16),
                        pltpu.VMEM((d, tf), BF16),
                        pltpu.VMEM((d, tf), BF16),
                        pltpu.VMEM((f, tf), BF16)],
    )
    return pl.pallas_call(
        _moe_kernel,
        grid_spec=grid_spec,
        out_shape=jax.ShapeDtypeStruct((r, d), BF16),
        compiler_params=_cparams(("arbitrary", "arbitrary")),
        name="moe_experts",
    )(block_e, nvalid, nused, x_sorted, w1, w1, b1.reshape(e, 1, f2), b1.reshape(e, 1, f2), w2,
      b2.reshape(e, 1, d), row_gate)


def _final_kernel(x1_ref, y_ref, gt_ref, g_ref, o_ref):
    d = x1_ref.shape[1]
    moe = y_ref[:, 0:d].astype(F32)
    for k in range(1, TOP_K):
        moe = moe + y_ref[:, k * d:(k + 1) * d].astype(F32)
    x2 = x1_ref[...] + gt_ref[0] * moe
    o_ref[...] = x2 * lax.rsqrt(jnp.mean(x2 * x2, axis=-1, keepdims=True) + EPS) * g_ref[...]


def _final(x1, y4, gt2, final_g, seq):
    t, d = x1.shape
    tm = 256
    per_b = seq // tm
    return pl.pallas_call(
        _final_kernel,
        grid=(t // tm,),
        in_specs=[pl.BlockSpec((tm, d), lambda i: (i, 0)),
                  pl.BlockSpec((tm, TOP_K * d), lambda i: (i, 0)),
                  pl.BlockSpec((1, 1, d), lambda i: (i // per_b, 0, 0)),
                  pl.BlockSpec((1, d), lambda i: (0, 0))],
        out_specs=pl.BlockSpec((tm, d), lambda i: (i, 0)),
        out_shape=jax.ShapeDtypeStruct((t, d), F32),
        compiler_params=_cparams(("arbitrary",)),
        name="combine_final_norm",
    )(x1, y4, gt2, final_g)


def _route(top_idx, gates, tm):
    k, t = top_idx.shape
    n = t * k
    flat_e = top_idx.T.reshape(n)
    flat_g = gates.T.reshape(n)
    order = jnp.argsort(flat_e, stable=True)
    se = flat_e[order]
    counts = jnp.zeros((N_EXPERTS,), jnp.int32).at[flat_e].add(1)
    starts = jnp.cumsum(counts) - counts
    padded = (counts + tm - 1) // tm * tm
    pad_ends = jnp.cumsum(padded)
    pad_starts = pad_ends - padded
    dest = pad_starts[se] + jnp.arange(n, dtype=jnp.int32) - starts[se]
    nblk = n // tm + N_EXPERTS
    nrows = nblk * tm
    row_tok = jnp.zeros((nrows,), jnp.int32).at[dest].set((order // k).astype(jnp.int32))
    row_gate = jnp.zeros((nrows,), F32).at[dest].set(flat_g[order])
    blk_start = jnp.arange(nblk, dtype=jnp.int32) * tm
    block_e = jnp.minimum(jnp.searchsorted(pad_ends, blk_start, side='right'), N_EXPERTS - 1).astype(jnp.int32)
    nvalid = jnp.clip(counts[block_e] - (blk_start - pad_starts[block_e]), 0, tm).astype(jnp.int32)
    nused = (pad_ends[-1] // tm).astype(jnp.int32).reshape(1)
    pos = jnp.zeros((n,), jnp.int32).at[order].set(dest)
    return row_tok, row_gate, block_e, nvalid, nused, pos


def kernel(x, c, positions, ada_w, ada_b, norm1_g, w_in, conv_w, gdn_a_log, gdn_dt_bias, gdn_norm_g, ret_norm_g,
           ret_norm_b, w_out, norm2_g, w_router, b_router, w1, b1, w2, b2, final_norm_g):
    batch, seq, d = x.shape
    t = batch * seq
    depth = ada_w.shape[0]
    xf = x.reshape(t, d)
    c8 = jnp.zeros((8, d), F32).at[:batch].set(c)
    half = RET_DK // 2
    inv_freq = jnp.power(ROPE_BASE, -jnp.linspace(0.0, 1.0, half, dtype=F32)).reshape(1, half)
    posb = jnp.broadcast_to(positions.astype(F32).reshape(t, 1), (t, half))

    zc = GDN_CONV_CH + GDN_V_W
    for l in range(depth):
        mod = _ada(c8, ada_w[l], ada_b[l])[:batch]
        sh1, sc1, gt1, sh2, sc2, gt2 = [m.reshape(batch, 1, d) for m in jnp.split(mod, 6, axis=-1)]

        wl = w_in[l]
        w_main = jnp.concatenate([wl[:, :zc], wl[:, zc + 2 * GDN_HEADS:]], axis=1).astype(BF16)
        w_ba = jnp.zeros((d, 256), F32)
        w_ba = w_ba.at[:, :GDN_HEADS].set(wl[:, zc:zc + GDN_HEADS])
        w_ba = w_ba.at[:, 128:128 + GDN_HEADS].set(wl[:, zc + GDN_HEADS:zc + 2 * GDN_HEADS]).astype(BF16)
        proj, ba = _inproj(xf, norm1_g[l].reshape(1, d), sc1, sh1, w_main, w_ba, seq)

        a_log128 = jnp.zeros((1, 128), F32).at[0, :GDN_HEADS].set(gdn_a_log[l])
        dtb128 = jnp.zeros((1, 128), F32).at[0, :GDN_HEADS].set(gdn_dt_bias[l])
        o_g = _gdn(proj, ba, conv_w[l], a_log128, dtb128, gdn_norm_g[l].reshape(1, GDN_DV), batch, seq)
        o_r = _ret(proj, posb, inv_freq, ret_norm_g[l].reshape(1, RET_W), ret_norm_b[l].reshape(1, RET_W),
                   batch, seq)

        wo = w_out[l].astype(BF16)
        x1, h2, top_idx, gates = _outproj(o_g, o_r, wo[:GDN_V_W], wo[GDN_V_W:], xf, gt1,
                                          norm2_g[l].reshape(1, d), sc2, sh2, w_router[l].T,
                                          b_router[l].reshape(N_EXPERTS, 1), seq)

        row_tok, row_gate, block_e, nvalid, nused, pos = _route(top_idx, gates, MOE_TM)
        x_sorted = h2[row_tok]
        y_sorted = _moe(block_e, nvalid, nused, x_sorted, w1[l], b1[l], w2[l], b2[l], row_gate.reshape(-1, 1))
        y4 = y_sorted[pos].reshape(t, TOP_K * d)
        last = l == depth - 1
        fin_g = final_norm_g.reshape(1, d) if last else jnp.ones((1, d), F32)
        xf = _final(x1, y4, gt2, fin_g, seq)
        assert last, "only depth 1 is supported"
    return xf.reshape(batch, seq, d)
```

```python
import functools
import math

import jax
import jax.numpy as jnp
from jax import lax
from jax.experimental import pallas as pl
from jax.experimental.pallas import tpu as pltpu

F32 = jnp.float32
BF16 = jnp.bfloat16
HIGHEST = lax.Precision.HIGHEST

EPS = 1e-6
GDN_HEADS = 8
GDN_DK = 128
GDN_DV = 128
CONV_WIDTH = 4
GDN_CHUNK = 64
GDN_NB = 2
RET_HEADS = 4
RET_DK = 256
RET_DV = 256
RET_CHUNK = 256
ROPE_BASE = 10000.0
N_EXPERTS = 32
TOP_K = 4
SWIGLU_LIMIT = 7.0
SWIGLU_ALPHA = 1.702

GDN_QK_W = GDN_HEADS * GDN_DK
GDN_V_W = GDN_HEADS * GDN_DV
GDN_CONV_CH = 2 * GDN_QK_W + GDN_V_W
RET_W = RET_HEADS * RET_DK

VMEM_LIMIT = 56 * 1024 * 1024

MOE_TM = 1024
MOE_SUB = 256
MOE_TF = 512
ROWS_PER_STEP = 2048


def _dot(a, b, precision=None):
    return jnp.dot(a, b, preferred_element_type=F32, precision=precision)


def _dot_nt(a, b, precision=None):
    return lax.dot_general(a, b, (((1,), (1,)), ((), ())), preferred_element_type=F32, precision=precision)


def _dot_tn(a, b, precision=None):
    return lax.dot_general(a, b, (((0,), (0,)), ((), ())), preferred_element_type=F32, precision=precision)


def _sigmoid(x):
    return 1.0 / (1.0 + jnp.exp(-x))


def _silu(x):
    return x * _sigmoid(x)


def _pack_bf16_pair(lo, hi):
    lo_bits = lax.bitcast_convert_type(lo.astype(BF16).astype(F32), jnp.uint32)
    hi_bits = lax.bitcast_convert_type(hi.astype(BF16).astype(F32), jnp.uint32)
    return (lo_bits >> 16) | hi_bits


def _unpack_bf16_pair(packed):
    lo = lax.bitcast_convert_type(packed << 16, F32)
    hi = lax.bitcast_convert_type(packed & jnp.uint32(0xFFFF0000), F32)
    return lo, hi


def _cparams(sem, vmem=VMEM_LIMIT):
    return pltpu.CompilerParams(dimension_semantics=sem, vmem_limit_bytes=vmem)


def _ada_kernel(c_ref, w_ref, b_ref, o_ref):
    cond = _silu(c_ref[...])
    o_ref[...] = _dot(cond, w_ref[...], precision=HIGHEST) + b_ref[...]


def _ada(c8, ada_w, ada_b):
    d, n = ada_w.shape
    tn = 1024
    return pl.pallas_call(
        _ada_kernel,
        grid=(n // tn,),
        in_specs=[pl.BlockSpec((8, d), lambda j: (0, 0)),
                  pl.BlockSpec((d, tn), lambda j: (0, j)),
                  pl.BlockSpec((1, tn), lambda j: (0, j))],
        out_specs=pl.BlockSpec((8, tn), lambda j: (0, j)),
        out_shape=jax.ShapeDtypeStruct((8, n), F32),
        compiler_params=_cparams(("arbitrary",)),
        name="ada_mod",
    )(c8, ada_w, ada_b.reshape(1, n))


def _inproj_kernel(x_ref, g_ref, sc_ref, sh_ref, w_ref, wba_ref, proj_ref, ba_ref, h_ref):
    @pl.when(pl.program_id(1) == 0)
    def _():
        x = x_ref[...]
        y = x * lax.rsqrt(jnp.mean(x * x, axis=-1, keepdims=True) + EPS)
        h = (y * g_ref[...]) * (1.0 + sc_ref[0]) + sh_ref[0]
        hb = h.astype(BF16)
        h_ref[...] = hb
        ba_ref[...] = _dot(hb, wba_ref[...])

    proj_ref[...] = _dot(h_ref[...], w_ref[...]).astype(BF16)


def _inproj(xf, g1, sc1, sh1, w_main, w_ba, seq):
    t, d = xf.shape
    n = w_main.shape[1]
    tm, tn = 1024, 1024
    per_b = seq // tm
    return pl.pallas_call(
        _inproj_kernel,
        grid=(t // tm, n // tn),
        in_specs=[pl.BlockSpec((tm, d), lambda i, j: (i, 0)),
                  pl.BlockSpec((1, d), lambda i, j: (0, 0)),
                  pl.BlockSpec((1, 1, d), lambda i, j: (i // per_b, 0, 0)),
                  pl.BlockSpec((1, 1, d), lambda i, j: (i // per_b, 0, 0)),
                  pl.BlockSpec((d, tn), lambda i, j: (0, j)),
                  pl.BlockSpec((d, 256), lambda i, j: (0, 0))],
        out_specs=[pl.BlockSpec((tm, tn), lambda i, j: (i, j)),
                   pl.BlockSpec((tm, 256), lambda i, j: (i, 0))],
        out_shape=[jax.ShapeDtypeStruct((t, n), BF16), jax.ShapeDtypeStruct((t, 256), F32)],
        scratch_shapes=[pltpu.VMEM((tm, d), BF16)],
        compiler_params=_cparams(("arbitrary", "arbitrary")),
        name="norm1_inproj",
    )(xf, g1, sc1, sh1, w_main, w_ba)


def _mm_each(xs, ys):
    return [_dot(x.astype(BF16), y.astype(BF16)) for x, y in zip(xs, ys)]


def _unit_lower_inverses(lms, n):
    r = lax.broadcasted_iota(jnp.int32, (n, n), 0)
    c = lax.broadcasted_iota(jnp.int32, (n, n), 1)
    eye = (r == c).astype(F32)
    diag_blk = (r // 8) == (c // 8)
    d1 = [jnp.where(diag_blk, lm, 0.0) for lm in lms]
    d2 = _mm_each(d1, d1)
    x = [eye - d for d in d1]
    xd = _mm_each(x, d2)
    d4 = _mm_each(d2, d2)
    x = [a + b for a, b in zip(x, xd)]
    xd = _mm_each(x, d4)
    x = [a + b for a, b in zip(x, xd)]
    s = 8
    while s < n:
        off_blk = ((r // (2 * s)) == (c // (2 * s))) & ((r // s) != (c // s))
        off = [jnp.where(off_blk, lm, 0.0) for lm in lms]
        xo = _mm_each(x, off)
        xox = _mm_each(xo, x)
        x = [a - b for a, b in zip(x, xox)]
        s *= 2
    return x


def _gdn_kernel(qkvz_ref, ba_ref, convw_ref, alog_ref, dtb_ref, ng_ref, o_ref, hist_ref, stage_ref, s_ref):
    C = GDN_CHUNK
    nb = qkvz_ref.shape[0]
    items = [(b, h) for b in range(nb) for h in range(GDN_HEADS)]

    @pl.when(pl.program_id(1) == 0)
    def _():
        hist_ref[...] = jnp.zeros_like(hist_ref)
        s_ref[...] = jnp.zeros_like(s_ref)

    r = lax.broadcasted_iota(jnp.int32, (C, C), 0)
    c = lax.broadcasted_iota(jnp.int32, (C, C), 1)
    incl = r >= c
    strict = r > c
    tri = incl.astype(F32)

    beta_all, gstep = [], []
    for b in range(nb):
        u_in = qkvz_ref[b, :, :GDN_CONV_CH].astype(F32)
        stage_ref[b, 0:8, :] = hist_ref[b]
        stage_ref[b, 8:8 + C, :] = u_in
        hist_ref[b] = u_in[C - 8:, :]
        beta_all.append(_sigmoid(ba_ref[b, :, :128]))
        a_in = ba_ref[b, :, 128:] + dtb_ref[...]
        softplus = jnp.maximum(a_in, 0.0) + jnp.log(1.0 + jnp.exp(-jnp.abs(a_in)))
        gstep.append(-jnp.exp(alog_ref[...]) * softplus)
    gcum = [_dot(tri, gs, precision=HIGHEST) for gs in gstep]
    gcum_t = [gc.T for gc in gcum]

    def conv(b, col0):
        acc = None
        for i in range(CONV_WIDTH):
            term = stage_ref[b, 5 + i:5 + i + C, col0:col0 + 128] * convw_ref[i:i + 1, col0:col0 + 128]
            acc = term if acc is None else acc + term
        return _silu(acc)

    qc = [conv(b, h * GDN_DK) for b, h in items]
    kc = [conv(b, GDN_QK_W + h * GDN_DK) for b, h in items]
    vc = [conv(b, 2 * GDN_QK_W + h * GDN_DV) for b, h in items]
    qn = [q * (lax.rsqrt(jnp.sum(q * q, axis=-1, keepdims=True) + EPS) * (GDN_DK ** -0.5)) for q in qc]
    kn = [k * lax.rsqrt(jnp.sum(k * k, axis=-1, keepdims=True) + EPS) for k in kc]
    kb = [k.astype(BF16) for k in kn]
    beta = [beta_all[b][:, h:h + 1] for b, h in items]
    g = [gcum[b][:, h:h + 1] for b, h in items]
    grow = [gcum_t[b][h:h + 1, :] for b, h in items]
    glast = [gcum[b][C - 1:C, h:h + 1] for b, h in items]
    dec_incl = [jnp.exp(jnp.where(incl, gi - gr, -1e30)) for gi, gr in zip(g, grow)]
    kk = [_dot_nt(k, k) for k in kb]
    lms = [bt * kki * jnp.where(strict, di, 0.0) for bt, kki, di in zip(beta, kk, dec_incl)]
    ainv = _unit_lower_inverses(lms, C)
    eg = [jnp.exp(gi) for gi in g]
    rhs = [jnp.concatenate([bt * v, (bt * e) * k], axis=1) for bt, v, e, k in zip(beta, vc, eg, kn)]
    sol = _mm_each(ainv, rhs)
    qk = [_dot_nt(q.astype(BF16), k) for q, k in zip(qn, kb)]
    aqk = [a * di for a, di in zip(qk, dec_incl)]
    q_dec = [q * e for q, e in zip(qn, eg)]
    k_end = [k * jnp.exp(gl - gi) for k, gl, gi in zip(kn, glast, g)]

    state = [s_ref[i] for i in range(len(items))]
    sb = [s.astype(BF16) for s in state]
    ws = [_dot(so[:, GDN_DV:].astype(BF16), s) for so, s in zip(sol, sb)]
    qs = [_dot(q.astype(BF16), s) for q, s in zip(q_dec, sb)]
    delta = [(so[:, :GDN_DV] - w).astype(BF16) for so, w in zip(sol, ws)]
    ad = [_dot(a.astype(BF16), d) for a, d in zip(aqk, delta)]
    kd = [_dot_tn(k.astype(BF16), d) for k, d in zip(k_end, delta)]
    for i, (b, h) in enumerate(items):
        s_ref[i] = jnp.exp(glast[i]) * state[i] + kd[i]
        out = qs[i] + ad[i]
        z = qkvz_ref[b, :, GDN_CONV_CH + h * GDN_DV:GDN_CONV_CH + (h + 1) * GDN_DV].astype(F32)
        o = out * lax.rsqrt(jnp.mean(out * out, axis=-1, keepdims=True) + EPS) * ng_ref[...] * _silu(z)
        o_ref[b, :, h * GDN_DV:(h + 1) * GDN_DV] = o.astype(BF16)


def _gdn(proj3, ba3, conv_w, a_log128, dtb128, norm_g):
    batch, seq, _ = proj3.shape
    C = GDN_CHUNK
    nb = GDN_NB if batch % GDN_NB == 0 else 1
    return pl.pallas_call(
        _gdn_kernel,
        grid=(batch // nb, seq // C),
        in_specs=[pl.BlockSpec((nb, C, 4096), lambda b, s: (b, s, 0)),
                  pl.BlockSpec((nb, C, 256), lambda b, s: (b, s, 0)),
                  pl.BlockSpec((CONV_WIDTH, GDN_CONV_CH), lambda b, s: (0, 0)),
                  pl.BlockSpec((1, 128), lambda b, s: (0, 0)),
                  pl.BlockSpec((1, 128), lambda b, s: (0, 0)),
                  pl.BlockSpec((1, 128), lambda b, s: (0, 0))],
        out_specs=pl.BlockSpec((nb, C, GDN_V_W), lambda b, s: (b, s, 0)),
        out_shape=jax.ShapeDtypeStruct((batch, seq, GDN_V_W), BF16),
        scratch_shapes=[pltpu.VMEM((nb, 8, GDN_CONV_CH), F32),
                        pltpu.VMEM((nb, C + 8, GDN_CONV_CH), F32),
                        pltpu.VMEM((nb * GDN_HEADS, GDN_DK, GDN_DV), F32)],
        compiler_params=_cparams(("arbitrary", "arbitrary")),
        name="gdn_heads",
    )(proj3, ba3, conv_w, a_log128, dtb128, norm_g)


def _ret_kernel(p_ref, pos_ref, invf_ref, ng_ref, nb_ref, o_ref, s_ref):
    C = RET_CHUNK

    @pl.when(pl.program_id(1) == 0)
    def _():
        s_ref[...] = jnp.zeros_like(s_ref)

    ang = pos_ref[...] * invf_ref[...]
    cs = jnp.cos(ang)
    sn = jnp.sin(ang)
    r = lax.broadcasted_iota(jnp.int32, (C, C), 0)
    c = lax.broadcasted_iota(jnp.int32, (C, C), 1)
    incl = r >= c
    dpos = (r - c).astype(F32)
    pcol = lax.broadcasted_iota(jnp.int32, (C, 1), 0).astype(F32)
    half = RET_DK // 2

    for h in range(RET_HEADS):
        lg = math.log1p(-(2.0 ** (-5.0 - h)))
        q0 = h * RET_DK
        k0 = RET_W + h * RET_DK
        v0 = 2 * RET_W + h * RET_DV
        g0 = 3 * RET_W + h * RET_DV
        q1 = p_ref[:, q0:q0 + half].astype(F32)
        q2 = p_ref[:, q0 + half:q0 + RET_DK].astype(F32)
        k1 = p_ref[:, k0:k0 + half].astype(F32)
        k2 = p_ref[:, k0 + half:k0 + RET_DK].astype(F32)
        q = jnp.concatenate([q1 * cs - q2 * sn, q2 * cs + q1 * sn], axis=1)
        k = jnp.concatenate([k1 * cs - k2 * sn, k2 * cs + k1 * sn], axis=1) * (RET_DK ** -0.5)
        v = p_ref[:, v0:v0 + RET_DV]
        dmat = jnp.exp(jnp.where(incl, dpos * lg, -1e30))
        scores = _dot_nt(q.astype(BF16), k.astype(BF16)) * dmat
        inner = _dot(scores.astype(BF16), v)
        q_dec = q * jnp.exp((pcol + 1.0) * lg)
        k_dec = k * jnp.exp((C - 1.0 - pcol) * lg)
        state = s_ref[h]
        out = inner + _dot(q_dec.astype(BF16), state.astype(BF16))
        s_ref[h] = math.exp(C * lg) * state + _dot_tn(k_dec.astype(BF16), v)
        mu = jnp.mean(out, axis=-1, keepdims=True)
        cen = out - mu
        var = jnp.mean(cen * cen, axis=-1, keepdims=True)
        on = cen * lax.rsqrt(var + EPS)
        gate = p_ref[:, g0:g0 + RET_DV].astype(F32)
        o = (on * ng_ref[:, h * RET_DV:(h + 1) * RET_DV] + nb_ref[:, h * RET_DV:(h + 1) * RET_DV]) * _silu(gate)
        o_ref[:, h * RET_DV:(h + 1) * RET_DV] = o.astype(BF16)


def _ret(proj, posb, inv_freq, norm_g, norm_b, batch, seq):
    t = proj.shape[0]
    C = RET_CHUNK
    nc = seq // C
    return pl.pallas_call(
        _ret_kernel,
        grid=(batch, nc),
        in_specs=[pl.BlockSpec((C, 4096), lambda b, s: (b * nc + s, 1)),
                  pl.BlockSpec((C, 128), lambda b, s: (b * nc + s, 0)),
                  pl.BlockSpec((1, 128), lambda b, s: (0, 0)),
                  pl.BlockSpec((1, RET_W), lambda b, s: (0, 0)),
                  pl.BlockSpec((1, RET_W), lambda b, s: (0, 0))],
        out_specs=pl.BlockSpec((C, RET_W), lambda b, s: (b * nc + s, 0)),
        out_shape=jax.ShapeDtypeStruct((t, RET_W), BF16),
        scratch_shapes=[pltpu.VMEM((RET_HEADS, RET_DK, RET_DV), F32)],
        compiler_params=_cparams(("arbitrary", "arbitrary")),
        name="ret_heads",
    )(proj, posb, inv_freq, norm_g, norm_b)


def _outproj_kernel(og_ref, or_ref, wt_ref, wb_ref, x_ref, gt_ref, g2_ref, sc_ref, sh_ref, wr_ref, br_ref,
                    x1_ref, h2_ref, idx_ref, gate_ref):
    mix = _dot(og_ref[...], wt_ref[...]) + _dot(or_ref[...], wb_ref[...])
    x1 = x_ref[...] + gt_ref[0] * mix
    x1_ref[...] = x1
    y = x1 * lax.rsqrt(jnp.mean(x1 * x1, axis=-1, keepdims=True) + EPS)
    h2 = (y * g2_ref[...]) * (1.0 + sc_ref[0]) + sh_ref[0]
    h2_ref[...] = _pack_bf16_pair(h2[:, :h2.shape[1] // 2], h2[:, h2.shape[1] // 2:])
    logits = _dot_nt(wr_ref[...], h2, precision=HIGHEST) + br_ref[...]
    eidx = lax.broadcasted_iota(jnp.int32, logits.shape, 0)
    vals = logits
    tops, idxs = [], []
    for _ in range(TOP_K):
        m = jnp.max(vals, axis=0, keepdims=True)
        sel = jnp.min(jnp.where(vals == m, eidx, N_EXPERTS), axis=0, keepdims=True)
        tops.append(m)
        idxs.append(sel)
        vals = jnp.where(eidx == sel, -jnp.inf, vals)
    exps = [jnp.exp(v - tops[0]) for v in tops]
    denom = exps[0] + exps[1] + exps[2] + exps[3]
    idx_ref[...] = jnp.concatenate(idxs, axis=0)
    gate_ref[...] = jnp.concatenate(exps, axis=0) / denom


def _outproj(o_g, o_r, w_top, w_bot, xf, gt1, g2, sc2, sh2, wr_t, br, seq):
    t, d = xf.shape
    tm = 512
    per_b = seq // tm
    row = lambda i: (i, 0)
    full = lambda i: (0, 0)
    perb = lambda i: (i // per_b, 0, 0)
    return pl.pallas_call(
        _outproj_kernel,
        grid=(t // tm,),
        in_specs=[pl.BlockSpec((tm, GDN_V_W), row),
                  pl.BlockSpec((tm, RET_W), row),
                  pl.BlockSpec((GDN_V_W, d), full),
                  pl.BlockSpec((RET_W, d), full),
                  pl.BlockSpec((tm, d), row),
                  pl.BlockSpec((1, 1, d), perb),
                  pl.BlockSpec((1, d), full),
                  pl.BlockSpec((1, 1, d), perb),
                  pl.BlockSpec((1, 1, d), perb),
                  pl.BlockSpec((N_EXPERTS, d), full),
                  pl.BlockSpec((N_EXPERTS, 1), full)],
        out_specs=[pl.BlockSpec((tm, d), row),
                   pl.BlockSpec((tm, d // 2), row),
                   pl.BlockSpec((TOP_K, tm), lambda i: (0, i)),
                   pl.BlockSpec((TOP_K, tm), lambda i: (0, i))],
        out_shape=[jax.ShapeDtypeStruct((t, d), F32), jax.ShapeDtypeStruct((t, d // 2), jnp.uint32),
                   jax.ShapeDtypeStruct((TOP_K, t), jnp.int32), jax.ShapeDtypeStruct((TOP_K, t), F32)],
        compiler_params=_cparams(("arbitrary",)),
        name="outproj_norm2_router",
    )(o_g, o_r, w_top, w_bot, xf, gt1, g2, sc2, sh2, wr_t, br)


ROW_GROUP = 8


def _row_move_kernel(nv_ref, src_ref, dst_ref, in_hbm, o_hbm, zero_ref, sem, zsem, *, fill_pad):
    n = src_ref.shape[-1]

    if fill_pad:
        @pl.when(pl.program_id(0) == 0)
        def _():
            zero_ref[...] = jnp.zeros_like(zero_ref)

            def slab(b, s):
                return pltpu.make_async_copy(zero_ref, o_hbm.at[pl.ds(b * MOE_TM + s * MOE_SUB, MOE_SUB)], zsem)

            def each_slab(b, carry, act):
                for s in range(MOE_TM // MOE_SUB):
                    @pl.when((s + 1) * MOE_SUB > nv_ref[b])
                    def _():
                        act(slab(b, s))
                return carry

            nblk = nv_ref.shape[0]
            lax.fori_loop(0, nblk, lambda b, c: each_slab(b, c, lambda cp: cp.start()), 0)
            lax.fori_loop(0, nblk, lambda b, c: each_slab(b, c, lambda cp: cp.wait()), 0)

    def row_copy(s, t):
        return pltpu.make_async_copy(in_hbm.at[pl.ds(s, 1)], o_hbm.at[pl.ds(t, 1)], sem)

    def group(g, carry, act):
        base = g * ROW_GROUP
        idx = [(src_ref[0, 0, base + u], dst_ref[0, 0, base + u]) for u in range(ROW_GROUP)]
        for s, t in idx:
            act(row_copy(s, t))
        return carry

    lax.fori_loop(0, n // ROW_GROUP, lambda g, c: group(g, c, lambda cp: cp.start()), 0)
    lax.fori_loop(0, n // ROW_GROUP, lambda g, c: group(g, c, lambda cp: cp.wait()), 0)


def _row_move(src, dst, rows_in, out_rows, nvalid=None):
    n = src.shape[0]
    d = rows_in.shape[1]
    steps = n // ROWS_PER_STEP
    fill_pad = nvalid is not None
    if nvalid is None:
        nvalid = jnp.zeros((1,), jnp.int32)
    idx_spec = pl.BlockSpec((1, 1, ROWS_PER_STEP), lambda i, nv: (i, 0, 0), memory_space=pltpu.SMEM)
    any_spec = pl.BlockSpec(memory_space=pl.ANY)
    return pl.pallas_call(
        functools.partial(_row_move_kernel, fill_pad=fill_pad),
        grid_spec=pltpu.PrefetchScalarGridSpec(
            num_scalar_prefetch=1,
            grid=(steps,),
            in_specs=[idx_spec, idx_spec, any_spec],
            out_specs=any_spec,
            scratch_shapes=[pltpu.VMEM((MOE_SUB if fill_pad else 8, d), rows_in.dtype),
                            pltpu.SemaphoreType.DMA(()), pltpu.SemaphoreType.DMA(())],
        ),
        out_shape=jax.ShapeDtypeStruct((out_rows, d), rows_in.dtype),
        compiler_params=_cparams(("arbitrary",)),
        name="row_gather" if fill_pad else "row_unsort",
    )(nvalid, src.reshape(steps, 1, ROWS_PER_STEP), dst.reshape(steps, 1, ROWS_PER_STEP), rows_in)


def _moe_kernel(be_ref, nv_ref, nu_ref, x_ref, w1g_ref, w1l_ref, b1g_ref, b1l_ref, w2_ref, b2_ref,
                o_ref, act_ref, wgb_ref, wlb_ref, w2b_ref, xb_ref):
    i = pl.program_id(0)
    j = pl.program_id(1)
    nj = act_ref.shape[0]
    tm = x_ref.shape[0]
    nv = nv_ref[i]
    half_out = o_ref.shape[1]

    @pl.when(jnp.logical_and(i < nu_ref[0], j == 0))
    def _():
        lo, hi = _unpack_bf16_pair(x_ref[...])
        xb_ref[...] = jnp.concatenate([lo, hi], axis=1).astype(BF16)

    def up(rows):
        xs = xb_ref[rows, :]
        hg = _dot(xs, wgb_ref[...]) + b1g_ref[0]
        hl = _dot(xs, wlb_ref[...]) + b1l_ref[0]
        hg = jnp.minimum(hg, SWIGLU_LIMIT)
        hl = jnp.clip(hl, -SWIGLU_LIMIT, SWIGLU_LIMIT)
        return (hg * _sigmoid(SWIGLU_ALPHA * hg) * (hl + 1.0)).astype(BF16)

    def down(rows):
        y = b2_ref[0] + _dot(act_ref[0, rows, :], w2b_ref[0:MOE_TF, :])
        for jj in range(1, nj):
            y = y + _dot(act_ref[jj, rows, :], w2b_ref[jj * MOE_TF:(jj + 1) * MOE_TF, :])
        return _pack_bf16_pair(y[:, :half_out], y[:, half_out:])

    @pl.when(jnp.logical_and(i < nu_ref[0], j < nj))
    def _():
        wgb_ref[...] = w1g_ref[0].astype(BF16)
        wlb_ref[...] = w1l_ref[0].astype(BF16)

        @pl.when(nv == tm)
        def _():
            act_ref[j] = up(slice(None))

        @pl.when(nv < tm)
        def _():
            for s in range(tm // MOE_SUB):
                rows = slice(s * MOE_SUB, (s + 1) * MOE_SUB)

                @pl.when(s * MOE_SUB < nv)
                def _():
                    act_ref[j, rows, :] = up(rows)

    @pl.when(jnp.logical_and(i >= nu_ref[0], j >= nj))
    def _():
        o_ref[...] = jnp.zeros(o_ref.shape, jnp.uint32)

    @pl.when(jnp.logical_and(i < nu_ref[0], j >= nj))
    def _():
        w2b_ref[...] = w2_ref[0].astype(BF16)

        @pl.when(nv == tm)
        def _():
            o_ref[...] = down(slice(None))

        @pl.when(nv < tm)
        def _():
            for s in range(tm // MOE_SUB):
                rows = slice(s * MOE_SUB, (s + 1) * MOE_SUB)

                @pl.when(s * MOE_SUB < nv)
                def _():
                    o_ref[rows, :] = down(rows)

                @pl.when(s * MOE_SUB >= nv)
                def _():
                    o_ref[rows, :] = jnp.zeros((MOE_SUB, half_out), jnp.uint32)


def _moe(block_e, nvalid, nused, x_sorted, w1, b1, w2, b2):
    r = x_sorted.shape[0]
    e, d, f2 = w1.shape
    f = f2 // 2
    tm, tf = MOE_TM, MOE_TF
    nblk = r // tm
    nj = f // tf
    assert d // tf == nj

    def live(i, nu):
        return jnp.minimum(i, nu[0] - 1)

    def c1(i, j, nu):
        return jnp.where(i < nu[0], jnp.minimum(j, nj - 1), nj - 1)

    def c2(i, j, nu):
        return jnp.where(i < nu[0], jnp.maximum(j - nj, 0), nj - 1)

    grid_spec = pltpu.PrefetchScalarGridSpec(
        num_scalar_prefetch=3,
        grid=(nblk, 2 * nj),
        in_specs=[
            pl.BlockSpec((tm, d // 2), lambda i, j, be, nv, nu: (live(i, nu), 0)),
            pl.BlockSpec((1, d, tf), lambda i, j, be, nv, nu: (be[live(i, nu)], 0, c1(i, j, nu))),
            pl.BlockSpec((1, d, tf), lambda i, j, be, nv, nu: (be[live(i, nu)], 0, nj + c1(i, j, nu))),
            pl.BlockSpec((1, 1, tf), lambda i, j, be, nv, nu: (be[live(i, nu)], 0, c1(i, j, nu))),
            pl.BlockSpec((1, 1, tf), lambda i, j, be, nv, nu: (be[live(i, nu)], 0, nj + c1(i, j, nu))),
            pl.BlockSpec((1, f, tf), lambda i, j, be, nv, nu: (be[live(i, nu)], 0, c2(i, j, nu))),
            pl.BlockSpec((1, 1, tf), lambda i, j, be, nv, nu: (be[live(i, nu)], 0, c2(i, j, nu))),
        ],
        out_specs=pl.BlockSpec((tm, tf // 2), lambda i, j, be, nv, nu: (i, jnp.maximum(j - nj, 0))),
        scratch_shapes=[pltpu.VMEM((nj, tm, tf), BF16),
                        pltpu.VMEM((d, tf), BF16),
                        pltpu.VMEM((d, tf), BF16),
                        pltpu.VMEM((f, tf), BF16),
                        pltpu.VMEM((tm, d), BF16)],
    )
    return pl.pallas_call(
        _moe_kernel,
        grid_spec=grid_spec,
        out_shape=jax.ShapeDtypeStruct((r, d // 2), jnp.uint32),
        compiler_params=_cparams(("arbitrary", "arbitrary")),
        name="moe_experts",
    )(block_e, nvalid, nused, x_sorted, w1, w1, b1.reshape(e, 1, f2), b1.reshape(e, 1, f2), w2,
      b2.reshape(e, 1, d))


def _final_kernel(x1_ref, y_ref, gk_ref, gt_ref, g_ref, o_ref):
    hw = MOE_TF // 2
    moe = None
    for k in range(TOP_K):
        lo, hi = _unpack_bf16_pair(y_ref[k])
        cols = []
        for jt in range(y_ref.shape[2] // hw):
            cols += [lo[:, jt * hw:(jt + 1) * hw], hi[:, jt * hw:(jt + 1) * hw]]
        term = gk_ref[:, k:k + 1] * jnp.concatenate(cols, axis=1)
        moe = term if moe is None else moe + term
    x2 = x1_ref[...] + gt_ref[0] * moe
    o_ref[...] = x2 * lax.rsqrt(jnp.mean(x2 * x2, axis=-1, keepdims=True) + EPS) * g_ref[...]


def _final(x1, y4, gates_tk, gt2, final_g, seq):
    t, d = x1.shape
    tm = 256
    per_b = seq // tm
    return pl.pallas_call(
        _final_kernel,
        grid=(t // tm,),
        in_specs=[pl.BlockSpec((tm, d), lambda i: (i, 0)),
                  pl.BlockSpec((TOP_K, tm, d // 2), lambda i: (0, i, 0)),
                  pl.BlockSpec((tm, TOP_K), lambda i: (i, 0)),
                  pl.BlockSpec((1, 1, d), lambda i: (i // per_b, 0, 0)),
                  pl.BlockSpec((1, d), lambda i: (0, 0))],
        out_specs=pl.BlockSpec((tm, d), lambda i: (i, 0)),
        out_shape=jax.ShapeDtypeStruct((t, d), F32),
        compiler_params=_cparams(("arbitrary",)),
        name="combine_final_norm",
    )(x1, y4, gates_tk, gt2, final_g)


def _route(top_idx, tm):
    k, t = top_idx.shape
    n = t * k
    flat_e = top_idx.T.reshape(n)
    se, order = lax.sort((flat_e, jnp.arange(n, dtype=jnp.int32)), num_keys=1, is_stable=True)
    experts = jnp.arange(N_EXPERTS, dtype=jnp.int32)
    onehot = se[:, None] == experts[None, :]
    counts = jnp.sum(onehot.astype(jnp.int32), axis=0)
    starts = jnp.cumsum(counts) - counts
    padded = (counts + tm - 1) // tm * tm
    pad_ends = jnp.cumsum(padded)
    pad_starts = pad_ends - padded
    shift = jnp.sum(jnp.where(onehot, (pad_starts - starts)[None, :], 0), axis=1)
    dst = jnp.arange(n, dtype=jnp.int32) + shift
    src_tok = order // k
    slot_major = (order % k) * t + order // k
    _, pos = lax.sort((slot_major, dst), num_keys=1)
    nblk = n // tm + N_EXPERTS
    blk_start = jnp.arange(nblk, dtype=jnp.int32) * tm
    block_e = jnp.minimum(jnp.sum((pad_ends[None, :] <= blk_start[:, None]).astype(jnp.int32), axis=1),
                          N_EXPERTS - 1)
    eh = block_e[:, None] == experts[None, :]
    cnt_b = jnp.sum(jnp.where(eh, counts[None, :], 0), axis=1)
    pst_b = jnp.sum(jnp.where(eh, pad_starts[None, :], 0), axis=1)
    nvalid = jnp.clip(cnt_b - (blk_start - pst_b), 0, tm).astype(jnp.int32)
    nused = (pad_ends[-1] // tm).astype(jnp.int32).reshape(1)
    return src_tok, dst, pos, block_e.astype(jnp.int32), nvalid, nused, nblk * tm


def kernel(x, c, positions, ada_w, ada_b, norm1_g, w_in, conv_w, gdn_a_log, gdn_dt_bias, gdn_norm_g, ret_norm_g,
           ret_norm_b, w_out, norm2_g, w_router, b_router, w1, b1, w2, b2, final_norm_g):
    batch, seq, d = x.shape
    t = batch * seq
    assert ada_w.shape[0] == 1, "the block is specified at depth 1"
    xf = x.reshape(t, d)
    c8 = jnp.zeros((8, d), F32).at[:batch].set(c)
    half = RET_DK // 2
    inv_freq = jnp.power(ROPE_BASE, -jnp.linspace(0.0, 1.0, half, dtype=F32)).reshape(1, half)
    posb = jnp.broadcast_to(positions.astype(F32).reshape(t, 1), (t, half))

    mod = _ada(c8, ada_w[0], ada_b[0])[:batch]
    sh1, sc1, gt1, sh2, sc2, gt2 = [m.reshape(batch, 1, d) for m in jnp.split(mod, 6, axis=-1)]

    zc = GDN_CONV_CH + GDN_V_W
    wl = w_in[0]
    w_main = jnp.concatenate([wl[:, :zc], wl[:, zc + 2 * GDN_HEADS:]], axis=1).astype(BF16)
    w_ba = jnp.zeros((d, 256), F32)
    w_ba = w_ba.at[:, :GDN_HEADS].set(wl[:, zc:zc + GDN_HEADS])
    w_ba = w_ba.at[:, 128:128 + GDN_HEADS].set(wl[:, zc + GDN_HEADS:zc + 2 * GDN_HEADS]).astype(BF16)
    proj, ba = _inproj(xf, norm1_g[0].reshape(1, d), sc1, sh1, w_main, w_ba, seq)

    a_log128 = jnp.zeros((1, 128), F32).at[0, :GDN_HEADS].set(gdn_a_log[0])
    dtb128 = jnp.zeros((1, 128), F32).at[0, :GDN_HEADS].set(gdn_dt_bias[0])
    o_g = _gdn(proj.reshape(batch, seq, -1), ba.reshape(batch, seq, -1), conv_w[0], a_log128, dtb128,
               gdn_norm_g[0].reshape(1, GDN_DV)).reshape(t, GDN_V_W)
    o_r = _ret(proj, posb, inv_freq, ret_norm_g[0].reshape(1, RET_W), ret_norm_b[0].reshape(1, RET_W), batch, seq)

    wo = w_out[0].astype(BF16)
    x1, h2, top_idx, gates = _outproj(o_g, o_r, wo[:GDN_V_W], wo[GDN_V_W:], xf, gt1, norm2_g[0].reshape(1, d),
                                      sc2, sh2, w_router[0].T, b_router[0].reshape(N_EXPERTS, 1), seq)

    src_tok, dst, pos, block_e, nvalid, nused, nrows = _route(top_idx, MOE_TM)
    x_sorted = _row_move(src_tok, dst, h2, nrows, nvalid=nvalid)
    y_sorted = _moe(block_e, nvalid, nused, x_sorted, w1[0], b1[0], w2[0], b2[0])
    y4 = _row_move(pos, jnp.arange(t * TOP_K, dtype=jnp.int32), y_sorted, t * TOP_K)
    out = _final(x1, y4.reshape(TOP_K, t, d // 2), gates.T, gt2, final_norm_g.reshape(1, d), seq)
    return out.reshape(batch, seq, d)
```

```python
import math

import jax
import jax.numpy as jnp
from jax import lax
from jax.experimental import pallas as pl
from jax.experimental.pallas import tpu as pltpu

F32 = jnp.float32
BF16 = jnp.bfloat16
HIGHEST = lax.Precision.HIGHEST

EPS = 1e-6
GDN_HEADS = 8
GDN_DK = 128
GDN_DV = 128
CONV_WIDTH = 4
GDN_CHUNK = 64
GDN_NB = 2
RET_HEADS = 4
RET_DK = 256
RET_DV = 256
RET_CHUNK = 256
ROPE_BASE = 10000.0
N_EXPERTS = 32
TOP_K = 4
SWIGLU_LIMIT = 7.0
SWIGLU_ALPHA = 1.702

GDN_QK_W = GDN_HEADS * GDN_DK
GDN_V_W = GDN_HEADS * GDN_DV
GDN_CONV_CH = 2 * GDN_QK_W + GDN_V_W
RET_W = RET_HEADS * RET_DK

VMEM_LIMIT = 56 * 1024 * 1024
MOE_VMEM_LIMIT = 60 * 1024 * 1024

MOE_TM = 2304
MOE_ROWS = tuple(range(256, MOE_TM + 1, 256))
MOE_TF = 256
GATHER_TM = 768
TOK_PIECE = 1024
FINAL_TM = 256


def _dot(a, b, precision=None):
    return jnp.dot(a, b, preferred_element_type=F32, precision=precision)


def _dot_nt(a, b, precision=None):
    return lax.dot_general(a, b, (((1,), (1,)), ((), ())), preferred_element_type=F32, precision=precision)


def _dot_tn(a, b, precision=None):
    return lax.dot_general(a, b, (((0,), (0,)), ((), ())), preferred_element_type=F32, precision=precision)


def _sigmoid(x):
    return 1.0 / (1.0 + jnp.exp(-x))


def _silu(x):
    return x * _sigmoid(x)


def _cparams(sem, vmem=VMEM_LIMIT):
    return pltpu.CompilerParams(dimension_semantics=sem, vmem_limit_bytes=vmem)


def _ada_kernel(c_ref, w_ref, b_ref, o_ref):
    cond = _silu(c_ref[...])
    o_ref[...] = _dot(cond, w_ref[...], precision=HIGHEST) + b_ref[...]


def _ada(c8, ada_w, ada_b):
    d, n = ada_w.shape
    tn = 1024
    return pl.pallas_call(
        _ada_kernel,
        grid=(n // tn,),
        in_specs=[pl.BlockSpec((8, d), lambda j: (0, 0)),
                  pl.BlockSpec((d, tn), lambda j: (0, j)),
                  pl.BlockSpec((1, tn), lambda j: (0, j))],
        out_specs=pl.BlockSpec((8, tn), lambda j: (0, j)),
        out_shape=jax.ShapeDtypeStruct((8, n), F32),
        compiler_params=_cparams(("arbitrary",)),
        name="ada_mod",
    )(c8, ada_w, ada_b.reshape(1, n))


def _inproj_kernel(x_ref, g_ref, sc_ref, sh_ref, w_ref, wba_ref, proj_ref, ba_ref, h_ref):
    @pl.when(pl.program_id(1) == 0)
    def _():
        x = x_ref[...]
        y = x * lax.rsqrt(jnp.mean(x * x, axis=-1, keepdims=True) + EPS)
        h = (y * g_ref[...]) * (1.0 + sc_ref[0]) + sh_ref[0]
        hb = h.astype(BF16)
        h_ref[...] = hb
        ba_ref[...] = _dot(hb, wba_ref[...])

    proj_ref[...] = _dot(h_ref[...], w_ref[...]).astype(BF16)


def _inproj(xf, g1, sc1, sh1, w_main, w_ba, seq):
    t, d = xf.shape
    n = w_main.shape[1]
    tm, tn = 1024, 1024
    per_b = seq // tm
    return pl.pallas_call(
        _inproj_kernel,
        grid=(t // tm, n // tn),
        in_specs=[pl.BlockSpec((tm, d), lambda i, j: (i, 0)),
                  pl.BlockSpec((1, d), lambda i, j: (0, 0)),
                  pl.BlockSpec((1, 1, d), lambda i, j: (i // per_b, 0, 0)),
                  pl.BlockSpec((1, 1, d), lambda i, j: (i // per_b, 0, 0)),
                  pl.BlockSpec((d, tn), lambda i, j: (0, j)),
                  pl.BlockSpec((d, 256), lambda i, j: (0, 0))],
        out_specs=[pl.BlockSpec((tm, tn), lambda i, j: (i, j)),
                   pl.BlockSpec((tm, 256), lambda i, j: (i, 0))],
        out_shape=[jax.ShapeDtypeStruct((t, n), BF16), jax.ShapeDtypeStruct((t, 256), F32)],
        scratch_shapes=[pltpu.VMEM((tm, d), BF16)],
        compiler_params=_cparams(("arbitrary", "arbitrary")),
        name="norm1_inproj",
    )(xf, g1, sc1, sh1, w_main, w_ba)


def _mm_each(xs, ys):
    return [_dot(x.astype(BF16), y.astype(BF16)) for x, y in zip(xs, ys)]


def _unit_lower_inverses(lms, n):
    r = lax.broadcasted_iota(jnp.int32, (n, n), 0)
    c = lax.broadcasted_iota(jnp.int32, (n, n), 1)
    eye = (r == c).astype(F32)
    diag_blk = (r // 8) == (c // 8)
    d1 = [jnp.where(diag_blk, lm, 0.0) for lm in lms]
    d2 = _mm_each(d1, d1)
    x = [eye - d for d in d1]
    xd = _mm_each(x, d2)
    d4 = _mm_each(d2, d2)
    x = [a + b for a, b in zip(x, xd)]
    xd = _mm_each(x, d4)
    x = [a + b for a, b in zip(x, xd)]
    s = 8
    while s < n:
        off_blk = ((r // (2 * s)) == (c // (2 * s))) & ((r // s) != (c // s))
        off = [jnp.where(off_blk, lm, 0.0) for lm in lms]
        xo = _mm_each(x, off)
        xox = _mm_each(xo, x)
        x = [a - b for a, b in zip(x, xox)]
        s *= 2
    return x


def _gdn_kernel(qkvz_ref, ba_ref, convw_ref, alog_ref, dtb_ref, ng_ref, o_ref, hist_ref, stage_ref, s_ref):
    C = GDN_CHUNK
    nb = qkvz_ref.shape[0]
    items = [(b, h) for b in range(nb) for h in range(GDN_HEADS)]

    @pl.when(pl.program_id(1) == 0)
    def _():
        hist_ref[...] = jnp.zeros_like(hist_ref)
        s_ref[...] = jnp.zeros_like(s_ref)

    r = lax.broadcasted_iota(jnp.int32, (C, C), 0)
    c = lax.broadcasted_iota(jnp.int32, (C, C), 1)
    incl = r >= c
    strict = r > c
    tri = incl.astype(F32)

    beta_all, gstep = [], []
    for b in range(nb):
        u_in = qkvz_ref[b, :, :GDN_CONV_CH].astype(F32)
        stage_ref[b, 0:8, :] = hist_ref[b]
        stage_ref[b, 8:8 + C, :] = u_in
        hist_ref[b] = u_in[C - 8:, :]
        beta_all.append(_sigmoid(ba_ref[b, :, :128]))
        a_in = ba_ref[b, :, 128:] + dtb_ref[...]
        softplus = jnp.maximum(a_in, 0.0) + jnp.log(1.0 + jnp.exp(-jnp.abs(a_in)))
        gstep.append(-jnp.exp(alog_ref[...]) * softplus)
    gcum = [_dot(tri, gs, precision=HIGHEST) for gs in gstep]
    gcum_t = [gc.T for gc in gcum]

    def conv(b, col0):
        acc = None
        for i in range(CONV_WIDTH):
            term = stage_ref[b, 5 + i:5 + i + C, col0:col0 + 128] * convw_ref[i:i + 1, col0:col0 + 128]
            acc = term if acc is None else acc + term
        return _silu(acc)

    qc = [conv(b, h * GDN_DK) for b, h in items]
    kc = [conv(b, GDN_QK_W + h * GDN_DK) for b, h in items]
    vc = [conv(b, 2 * GDN_QK_W + h * GDN_DV) for b, h in items]
    qn = [q * (lax.rsqrt(jnp.sum(q * q, axis=-1, keepdims=True) + EPS) * (GDN_DK ** -0.5)) for q in qc]
    kn = [k * lax.rsqrt(jnp.sum(k * k, axis=-1, keepdims=True) + EPS) for k in kc]
    kb = [k.astype(BF16) for k in kn]
    beta = [beta_all[b][:, h:h + 1] for b, h in items]
    g = [gcum[b][:, h:h + 1] for b, h in items]
    grow = [gcum_t[b][h:h + 1, :] for b, h in items]
    glast = [gcum[b][C - 1:C, h:h + 1] for b, h in items]
    dec_incl = [jnp.exp(jnp.where(incl, gi - gr, -1e30)) for gi, gr in zip(g, grow)]
    kk = [_dot_nt(k, k) for k in kb]
    lms = [bt * kki * jnp.where(strict, di, 0.0) for bt, kki, di in zip(beta, kk, dec_incl)]
    ainv = _unit_lower_inverses(lms, C)
    eg = [jnp.exp(gi) for gi in g]
    rhs = [jnp.concatenate([bt * v, (bt * e) * k], axis=1) for bt, v, e, k in zip(beta, vc, eg, kn)]
    sol = _mm_each(ainv, rhs)
    qk = [_dot_nt(q.astype(BF16), k) for q, k in zip(qn, kb)]
    aqk = [a * di for a, di in zip(qk, dec_incl)]
    q_dec = [q * e for q, e in zip(qn, eg)]
    k_end = [k * jnp.exp(gl - gi) for k, gl, gi in zip(kn, glast, g)]

    state = [s_ref[i] for i in range(len(items))]
    sb = [s.astype(BF16) for s in state]
    ws = [_dot(so[:, GDN_DV:].astype(BF16), s) for so, s in zip(sol, sb)]
    qs = [_dot(q.astype(BF16), s) for q, s in zip(q_dec, sb)]
    delta = [(so[:, :GDN_DV] - w).astype(BF16) for so, w in zip(sol, ws)]
    ad = [_dot(a.astype(BF16), d) for a, d in zip(aqk, delta)]
    kd = [_dot_tn(k.astype(BF16), d) for k, d in zip(k_end, delta)]
    for i, (b, h) in enumerate(items):
        s_ref[i] = jnp.exp(glast[i]) * state[i] + kd[i]
        out = qs[i] + ad[i]
        z = qkvz_ref[b, :, GDN_CONV_CH + h * GDN_DV:GDN_CONV_CH + (h + 1) * GDN_DV].astype(F32)
        o = out * lax.rsqrt(jnp.mean(out * out, axis=-1, keepdims=True) + EPS) * ng_ref[...] * _silu(z)
        o_ref[b, :, h * GDN_DV:(h + 1) * GDN_DV] = o.astype(BF16)


def _gdn(proj3, ba3, conv_w, a_log128, dtb128, norm_g):
    batch, seq, _ = proj3.shape
    C = GDN_CHUNK
    nb = GDN_NB if batch % GDN_NB == 0 else 1
    return pl.pallas_call(
        _gdn_kernel,
        grid=(batch // nb, seq // C),
        in_specs=[pl.BlockSpec((nb, C, 4096), lambda b, s: (b, s, 0)),
                  pl.BlockSpec((nb, C, 256), lambda b, s: (b, s, 0)),
                  pl.BlockSpec((CONV_WIDTH, GDN_CONV_CH), lambda b, s: (0, 0)),
                  pl.BlockSpec((1, 128), lambda b, s: (0, 0)),
                  pl.BlockSpec((1, 128), lambda b, s: (0, 0)),
                  pl.BlockSpec((1, 128), lambda b, s: (0, 0))],
        out_specs=pl.BlockSpec((nb, C, GDN_V_W), lambda b, s: (b, s, 0)),
        out_shape=jax.ShapeDtypeStruct((batch, seq, GDN_V_W), BF16),
        scratch_shapes=[pltpu.VMEM((nb, 8, GDN_CONV_CH), F32),
                        pltpu.VMEM((nb, C + 8, GDN_CONV_CH), F32),
                        pltpu.VMEM((nb * GDN_HEADS, GDN_DK, GDN_DV), F32)],
        compiler_params=_cparams(("arbitrary", "arbitrary")),
        name="gdn_heads",
    )(proj3, ba3, conv_w, a_log128, dtb128, norm_g)


def _ret_kernel(p_ref, pos_ref, invf_ref, ng_ref, nb_ref, o_ref, s_ref):
    C = RET_CHUNK

    @pl.when(pl.program_id(1) == 0)
    def _():
        s_ref[...] = jnp.zeros_like(s_ref)

    ang = pos_ref[...] * invf_ref[...]
    cs = jnp.cos(ang)
    sn = jnp.sin(ang)
    r = lax.broadcasted_iota(jnp.int32, (C, C), 0)
    c = lax.broadcasted_iota(jnp.int32, (C, C), 1)
    incl = r >= c
    dpos = (r - c).astype(F32)
    pcol = lax.broadcasted_iota(jnp.int32, (C, 1), 0).astype(F32)
    half = RET_DK // 2

    for h in range(RET_HEADS):
        lg = math.log1p(-(2.0 ** (-5.0 - h)))
        q0 = h * RET_DK
        k0 = RET_W + h * RET_DK
        v0 = 2 * RET_W + h * RET_DV
        g0 = 3 * RET_W + h * RET_DV
        q1 = p_ref[:, q0:q0 + half].astype(F32)
        q2 = p_ref[:, q0 + half:q0 + RET_DK].astype(F32)
        k1 = p_ref[:, k0:k0 + half].astype(F32)
        k2 = p_ref[:, k0 + half:k0 + RET_DK].astype(F32)
        q = jnp.concatenate([q1 * cs - q2 * sn, q2 * cs + q1 * sn], axis=1)
        k = jnp.concatenate([k1 * cs - k2 * sn, k2 * cs + k1 * sn], axis=1) * (RET_DK ** -0.5)
        v = p_ref[:, v0:v0 + RET_DV]
        dmat = jnp.exp(jnp.where(incl, dpos * lg, -1e30))
        scores = _dot_nt(q.astype(BF16), k.astype(BF16)) * dmat
        inner = _dot(scores.astype(BF16), v)
        q_dec = q * jnp.exp((pcol + 1.0) * lg)
        k_dec = k * jnp.exp((C - 1.0 - pcol) * lg)
        state = s_ref[h]
        out = inner + _dot(q_dec.astype(BF16), state.astype(BF16))
        s_ref[h] = math.exp(C * lg) * state + _dot_tn(k_dec.astype(BF16), v)
        mu = jnp.mean(out, axis=-1, keepdims=True)
        cen = out - mu
        var = jnp.mean(cen * cen, axis=-1, keepdims=True)
        on = cen * lax.rsqrt(var + EPS)
        gate = p_ref[:, g0:g0 + RET_DV].astype(F32)
        o = (on * ng_ref[:, h * RET_DV:(h + 1) * RET_DV] + nb_ref[:, h * RET_DV:(h + 1) * RET_DV]) * _silu(gate)
        o_ref[:, h * RET_DV:(h + 1) * RET_DV] = o.astype(BF16)


def _ret(proj, posb, inv_freq, norm_g, norm_b, batch, seq):
    t = proj.shape[0]
    C = RET_CHUNK
    nc = seq // C
    return pl.pallas_call(
        _ret_kernel,
        grid=(batch, nc),
        in_specs=[pl.BlockSpec((C, 4096), lambda b, s: (b * nc + s, 1)),
                  pl.BlockSpec((C, 128), lambda b, s: (b * nc + s, 0)),
                  pl.BlockSpec((1, 128), lambda b, s: (0, 0)),
                  pl.BlockSpec((1, RET_W), lambda b, s: (0, 0)),
                  pl.BlockSpec((1, RET_W), lambda b, s: (0, 0))],
        out_specs=pl.BlockSpec((C, RET_W), lambda b, s: (b * nc + s, 0)),
        out_shape=jax.ShapeDtypeStruct((t, RET_W), BF16),
        scratch_shapes=[pltpu.VMEM((RET_HEADS, RET_DK, RET_DV), F32)],
        compiler_params=_cparams(("arbitrary", "arbitrary")),
        name="ret_heads",
    )(proj, posb, inv_freq, norm_g, norm_b)


def _outproj_kernel(og_ref, or_ref, wt_ref, wb_ref, x_ref, gt_ref, g2_ref, sc_ref, sh_ref, wr_ref, br_ref,
                    x1_ref, h2_ref, idx_ref, gate_ref):
    mix = _dot(og_ref[...], wt_ref[...]) + _dot(or_ref[...], wb_ref[...])
    x1 = x_ref[...] + gt_ref[0] * mix
    x1_ref[...] = x1
    y = x1 * lax.rsqrt(jnp.mean(x1 * x1, axis=-1, keepdims=True) + EPS)
    h2 = (y * g2_ref[...]) * (1.0 + sc_ref[0]) + sh_ref[0]
    h2_ref[...] = h2
    logits = _dot_nt(wr_ref[...], h2, precision=HIGHEST) + br_ref[...]
    eidx = lax.broadcasted_iota(jnp.int32, logits.shape, 0)
    vals = logits
    tops, idxs = [], []
    for _ in range(TOP_K):
        m = jnp.max(vals, axis=0, keepdims=True)
        sel = jnp.min(jnp.where(vals == m, eidx, N_EXPERTS), axis=0, keepdims=True)
        tops.append(m)
        idxs.append(sel)
        vals = jnp.where(eidx == sel, -jnp.inf, vals)
    exps = [jnp.exp(v - tops[0]) for v in tops]
    denom = exps[0] + exps[1] + exps[2] + exps[3]
    idx_ref[...] = jnp.concatenate(idxs, axis=0)
    gate_ref[...] = jnp.concatenate(exps, axis=0) / denom


def _outproj(o_g, o_r, w_top, w_bot, xf, gt1, g2, sc2, sh2, wr_t, br, seq):
    t, d = xf.shape
    tm = 512
    per_b = seq // tm
    row = lambda i: (i, 0)
    full = lambda i: (0, 0)
    perb = lambda i: (i // per_b, 0, 0)
    return pl.pallas_call(
        _outproj_kernel,
        grid=(t // tm,),
        in_specs=[pl.BlockSpec((tm, GDN_V_W), row),
                  pl.BlockSpec((tm, RET_W), row),
                  pl.BlockSpec((GDN_V_W, d), full),
                  pl.BlockSpec((RET_W, d), full),
                  pl.BlockSpec((tm, d), row),
                  pl.BlockSpec((1, 1, d), perb),
                  pl.BlockSpec((1, d), full),
                  pl.BlockSpec((1, 1, d), perb),
                  pl.BlockSpec((1, 1, d), perb),
                  pl.BlockSpec((N_EXPERTS, d), full),
                  pl.BlockSpec((N_EXPERTS, 1), full)],
        out_specs=[pl.BlockSpec((tm, d), row),
                   pl.BlockSpec((tm, d), row),
                   pl.BlockSpec((TOP_K, tm), lambda i: (0, i)),
                   pl.BlockSpec((TOP_K, tm), lambda i: (0, i))],
        out_shape=[jax.ShapeDtypeStruct((t, d), F32), jax.ShapeDtypeStruct((t, d), F32),
                   jax.ShapeDtypeStruct((TOP_K, t), jnp.int32), jax.ShapeDtypeStruct((TOP_K, t), F32)],
        compiler_params=_cparams(("arbitrary",)),
        name="outproj_norm2_router",
    )(o_g, o_r, w_top, w_bot, xf, gt1, g2, sc2, sh2, wr_t, br)


ROW_GROUP = 8


def _start_copy(copy, parity):
    copy.start(priority=parity)


def _wait_copy(copy, parity):
    copy.wait()


def _rows_to_vmem(index_of, n_rows, rows_hbm, dst_ref, sem, act):
    def group(g, carry):
        idx = [index_of(g * ROW_GROUP + u) for u in range(ROW_GROUP)]
        for u, s in enumerate(idx):
            act(pltpu.make_async_copy(rows_hbm.at[pl.ds(s, 1)], dst_ref.at[g, pl.ds(u, 1)], sem), u % 2)
        return carry

    lax.fori_loop(0, (n_rows + ROW_GROUP - 1) // ROW_GROUP, group, 0)


def _gather_kernel(nv_ref, first_ref, nu_ref, tok_ref, tok_next_ref, h_hbm, o_ref, stage_ref, sem):
    g = pl.program_id(0)
    slot = g % 2
    gm = o_ref.shape[0]

    def fetch(window_ref, piece, half, act):
        first = first_ref[piece]
        off = first - jnp.minimum(first // TOK_PIECE, nu_ref[1] - 1) * TOK_PIECE
        _rows_to_vmem(lambda u: window_ref[0, 0, off + u], nv_ref[piece], h_hbm, stage_ref.at[half], sem.at[half], act)

    @pl.when(g == 0)
    def _():
        stage_ref[...] = jnp.zeros_like(stage_ref)
        fetch(tok_ref, 0, 0, _start_copy)

    @pl.when(g + 1 < pl.num_programs(0))
    def _():
        fetch(tok_next_ref, g + 1, 1 - slot, _start_copy)

    fetch(tok_ref, g, slot, _wait_copy)

    @pl.when(g < nu_ref[0])
    def _():
        rows = stage_ref[slot].reshape(gm, o_ref.shape[1])
        real = lax.broadcasted_iota(jnp.int32, (gm, 1), 0) < nv_ref[g]
        o_ref[...] = jnp.where(real, rows, 0.0).astype(o_ref.dtype)


def _gather(nv_piece, first_piece, n_used_pieces, src_tok, h2):
    n = src_tok.shape[0]
    n_tok_pieces = n // TOK_PIECE
    pieces = nv_piece.shape[0]
    dh = h2.shape[1]
    gm = GATHER_TM
    tok = src_tok.reshape(n_tok_pieces, TOK_PIECE)
    tok_after = jnp.concatenate([tok[1:], jnp.zeros((1, TOK_PIECE), jnp.int32)], axis=0)
    windows = jnp.concatenate([tok, tok_after], axis=1).reshape(n_tok_pieces, 1, 2 * TOK_PIECE)
    counts = jnp.concatenate([n_used_pieces, jnp.full((1,), n_tok_pieces, jnp.int32)])

    def window(g, fr):
        return jnp.minimum(fr[g] // TOK_PIECE, n_tok_pieces - 1)

    return pl.pallas_call(
        _gather_kernel,
        grid_spec=pltpu.PrefetchScalarGridSpec(
            num_scalar_prefetch=3,
            grid=(pieces,),
            in_specs=[pl.BlockSpec((1, 1, 2 * TOK_PIECE), lambda g, nv, fr, nu: (window(g, fr), 0, 0),
                                   memory_space=pltpu.SMEM),
                      pl.BlockSpec((1, 1, 2 * TOK_PIECE),
                                   lambda g, nv, fr, nu: (window(jnp.minimum(g + 1, pieces - 1), fr), 0, 0),
                                   memory_space=pltpu.SMEM),
                      pl.BlockSpec(memory_space=pl.ANY)],
            out_specs=pl.BlockSpec((gm, dh), lambda g, nv, fr, nu: (jnp.minimum(g, nu[0] - 1), 0)),
            scratch_shapes=[pltpu.VMEM((2, gm // ROW_GROUP, ROW_GROUP, dh), h2.dtype),
                            pltpu.SemaphoreType.DMA((2,))],
        ),
        out_shape=jax.ShapeDtypeStruct((pieces * gm, dh), BF16),
        compiler_params=_cparams(("arbitrary",)),
        name="row_gather",
    )(nv_piece, first_piece, counts, windows, windows, h2)


def _moe_kernel(be_ref, nv_ref, nu_ref, x_ref, w1g_ref, w1l_ref, b1g_ref, b1l_ref, w2_ref, b2_ref,
                o_ref, act_ref):
    i = pl.program_id(0)
    j = pl.program_id(1)
    nj = act_ref.shape[0]
    tm = x_ref.shape[0]
    nv = nv_ref[i]
    used = i < nu_ref[0]

    def up(m):
        xs = x_ref[0:m, :]
        hg = _dot(xs, w1g_ref[0].astype(BF16)) + b1g_ref[0]
        hl = _dot(xs, w1l_ref[0].astype(BF16)) + b1l_ref[0]
        hg = jnp.minimum(hg, SWIGLU_LIMIT)
        hl = jnp.clip(hl, -SWIGLU_LIMIT, SWIGLU_LIMIT)
        return (hg * _sigmoid(SWIGLU_ALPHA * hg) * (hl + 1.0)).astype(BF16)

    def down(m):
        y = b2_ref[0] + _dot(act_ref[0, 0:m, :], w2_ref[0, 0:MOE_TF, :].astype(BF16))
        for jj in range(1, nj):
            y = y + _dot(act_ref[jj, 0:m, :], w2_ref[0, jj * MOE_TF:(jj + 1) * MOE_TF, :].astype(BF16))
        return y

    lo = 0
    for m in MOE_ROWS:
        fits = jnp.logical_and(used, jnp.logical_and(nv > lo, nv <= m))

        @pl.when(jnp.logical_and(fits, j < nj))
        def _():
            act_ref[j, 0:m, :] = up(m)

        @pl.when(jnp.logical_and(fits, j >= nj))
        def _():
            o_ref[0:m, :] = down(m)
            if m < tm:
                o_ref[m:tm, :] = jnp.zeros((tm - m, o_ref.shape[1]), o_ref.dtype)

        lo = m


def _moe(block_e, nvalid, nused, x_sorted, w1, b1, w2, b2):
    r = x_sorted.shape[0]
    e, d, f2 = w1.shape
    f = f2 // 2
    tm, tf = MOE_TM, MOE_TF
    assert MOE_ROWS[-1] == tm
    nblk = r // tm
    nj = f // tf
    assert d // tf == nj

    def live(i, nu):
        return jnp.minimum(i, nu[0] - 1)

    def c1(i, j, nu):
        return jnp.where(i < nu[0], jnp.minimum(j, nj - 1), nj - 1)

    def c2(i, j, nu):
        return jnp.where(i < nu[0], jnp.maximum(j - nj, 0), nj - 1)

    grid_spec = pltpu.PrefetchScalarGridSpec(
        num_scalar_prefetch=3,
        grid=(nblk, 2 * nj),
        in_specs=[
            pl.BlockSpec((tm, d), lambda i, j, be, nv, nu: (live(i, nu), 0)),
            pl.BlockSpec((1, d, tf), lambda i, j, be, nv, nu: (be[live(i, nu)], 0, c1(i, j, nu))),
            pl.BlockSpec((1, d, tf), lambda i, j, be, nv, nu: (be[live(i, nu)], 0, nj + c1(i, j, nu))),
            pl.BlockSpec((1, 1, tf), lambda i, j, be, nv, nu: (be[live(i, nu)], 0, c1(i, j, nu))),
            pl.BlockSpec((1, 1, tf), lambda i, j, be, nv, nu: (be[live(i, nu)], 0, nj + c1(i, j, nu))),
            pl.BlockSpec((1, f, tf), lambda i, j, be, nv, nu: (be[live(i, nu)], 0, c2(i, j, nu))),
            pl.BlockSpec((1, 1, tf), lambda i, j, be, nv, nu: (be[live(i, nu)], 0, c2(i, j, nu))),
        ],
        out_specs=pl.BlockSpec((tm, tf), lambda i, j, be, nv, nu: (live(i, nu), c2(i, j, nu))),
        scratch_shapes=[pltpu.VMEM((nj, tm, tf), BF16)],
    )
    return pl.pallas_call(
        _moe_kernel,
        grid_spec=grid_spec,
        out_shape=jax.ShapeDtypeStruct((r, d), F32),
        compiler_params=_cparams(("arbitrary", "arbitrary"), vmem=MOE_VMEM_LIMIT),
        name="moe_experts",
    )(block_e, nvalid, nused, x_sorted, w1, w1, b1.reshape(e, 1, f2), b1.reshape(e, 1, f2), w2,
      b2.reshape(e, 1, d))


def _final_kernel(pos_ref, pos_next_ref, x1_ref, gk_ref, gt_ref, g_ref, y_hbm, o_ref, buf_ref, sem):
    i = pl.program_id(0)
    slot = i % 2
    tm = x1_ref.shape[0]
    n_rows = TOP_K * tm

    this_tile = lambda u: pos_ref[0, 0, u]
    next_tile = lambda u: pos_next_ref[0, 0, u]

    @pl.when(i == 0)
    def _():
        _rows_to_vmem(this_tile, n_rows, y_hbm, buf_ref.at[0], sem.at[0], _start_copy)

    @pl.when(i + 1 < pl.num_programs(0))
    def _():
        _rows_to_vmem(next_tile, n_rows, y_hbm, buf_ref.at[1 - slot], sem.at[1 - slot], _start_copy)

    _rows_to_vmem(this_tile, n_rows, y_hbm, buf_ref.at[slot], sem.at[slot], _wait_copy)

    per_slot = tm // ROW_GROUP
    moe = None
    for k in range(TOP_K):
        yk = buf_ref[slot, k * per_slot:(k + 1) * per_slot].reshape(tm, buf_ref.shape[3])
        term = gk_ref[:, k:k + 1] * yk
        moe = term if moe is None else moe + term
    x2 = x1_ref[...] + gt_ref[0] * moe
    o_ref[...] = x2 * lax.rsqrt(jnp.mean(x2 * x2, axis=-1, keepdims=True) + EPS) * g_ref[...]


def _final(pos_tiles, x1, y_sorted, gates_tk, gt2, final_g, seq):
    t, d = x1.shape
    nt, _, n_rows = pos_tiles.shape
    tm = n_rows // TOP_K
    per_b = seq // tm
    return pl.pallas_call(
        _final_kernel,
        grid=(nt,),
        in_specs=[pl.BlockSpec((1, 1, n_rows), lambda i: (i, 0, 0), memory_space=pltpu.SMEM),
                  pl.BlockSpec((1, 1, n_rows), lambda i: (jnp.minimum(i + 1, nt - 1), 0, 0),
                               memory_space=pltpu.SMEM),
                  pl.BlockSpec((tm, d), lambda i: (i, 0)),
                  pl.BlockSpec((tm, TOP_K), lambda i: (i, 0)),
                  pl.BlockSpec((1, 1, d), lambda i: (i // per_b, 0, 0)),
                  pl.BlockSpec((1, d), lambda i: (0, 0)),
                  pl.BlockSpec(memory_space=pl.ANY)],
        out_specs=pl.BlockSpec((tm, d), lambda i: (i, 0)),
        out_shape=jax.ShapeDtypeStruct((t, d), F32),
        scratch_shapes=[pltpu.VMEM((2, n_rows // ROW_GROUP, ROW_GROUP, y_sorted.shape[1]), y_sorted.dtype),
                        pltpu.SemaphoreType.DMA((2,))],
        compiler_params=_cparams(("arbitrary",)),
        name="combine_final_norm",
    )(pos_tiles, pos_tiles, x1, gates_tk, gt2, final_g, y_sorted)


def _route(top_idx, tm):
    k, t = top_idx.shape
    n = t * k
    flat_e = top_idx.T.reshape(n)
    se, order = lax.sort((flat_e, jnp.arange(n, dtype=jnp.int32)), num_keys=1, is_stable=True)
    experts = jnp.arange(N_EXPERTS, dtype=jnp.int32)
    onehot = se[:, None] == experts[None, :]
    counts = jnp.sum(onehot.astype(jnp.int32), axis=0)
    starts = jnp.cumsum(counts) - counts
    nb_e = (counts + tm - 1) // tm
    per_blk = (counts + jnp.maximum(nb_e, 1) * 8 - 1) // (jnp.maximum(nb_e, 1) * 8) * 8
    blk_ends = jnp.cumsum(nb_e)
    blk_starts = blk_ends - nb_e

    def of_expert(table, hot):
        return jnp.sum(jnp.where(hot, table[None, :], 0), axis=1)

    rank = jnp.arange(n, dtype=jnp.int32) - of_expert(starts, onehot)
    per_s = jnp.maximum(of_expert(per_blk, onehot), 1)
    jblk = rank // per_s
    dst = (of_expert(blk_starts, onehot) + jblk) * tm + rank - jblk * per_s
    src_tok = order // k
    slot_major = (order % k) * t + order // k
    _, pos = lax.sort((slot_major, dst), num_keys=1)
    nblk = n // tm + N_EXPERTS
    blk = jnp.arange(nblk, dtype=jnp.int32)
    block_e = jnp.minimum(jnp.sum((blk_ends[None, :] <= blk[:, None]).astype(jnp.int32), axis=1), N_EXPERTS - 1)
    eh = block_e[:, None] == experts[None, :]
    per_b = of_expert(per_blk, eh)
    done = (blk - of_expert(blk_starts, eh)) * per_b
    nvalid = jnp.clip(of_expert(counts, eh) - done, 0, per_b).astype(jnp.int32)
    first = of_expert(starts, eh) + done
    nused = blk_ends[-1].astype(jnp.int32).reshape(1)
    per = tm // GATHER_TM
    sub = jnp.arange(per, dtype=jnp.int32)[None, :] * GATHER_TM
    nv_piece = jnp.clip(nvalid[:, None] - sub, 0, GATHER_TM).astype(jnp.int32).reshape(nblk * per)
    first_piece = jnp.clip(first[:, None] + sub, 0, n).astype(jnp.int32).reshape(nblk * per)
    return src_tok, pos, block_e.astype(jnp.int32), nvalid, nused, nv_piece, first_piece


def kernel(x, c, positions, ada_w, ada_b, norm1_g, w_in, conv_w, gdn_a_log, gdn_dt_bias, gdn_norm_g, ret_norm_g,
           ret_norm_b, w_out, norm2_g, w_router, b_router, w1, b1, w2, b2, final_norm_g):
    batch, seq, d = x.shape
    t = batch * seq
    assert ada_w.shape[0] == 1, "the block is specified at depth 1"
    xf = x.reshape(t, d)
    c8 = jnp.zeros((8, d), F32).at[:batch].set(c)
    half = RET_DK // 2
    inv_freq = jnp.power(ROPE_BASE, -jnp.linspace(0.0, 1.0, half, dtype=F32)).reshape(1, half)
    posb = jnp.broadcast_to(positions.astype(F32).reshape(t, 1), (t, half))

    mod = _ada(c8, ada_w[0], ada_b[0])[:batch]
    sh1, sc1, gt1, sh2, sc2, gt2 = [m.reshape(batch, 1, d) for m in jnp.split(mod, 6, axis=-1)]

    zc = GDN_CONV_CH + GDN_V_W
    wl = w_in[0]
    w_main = jnp.concatenate([wl[:, :zc], wl[:, zc + 2 * GDN_HEADS:]], axis=1).astype(BF16)
    w_ba = jnp.zeros((d, 256), F32)
    w_ba = w_ba.at[:, :GDN_HEADS].set(wl[:, zc:zc + GDN_HEADS])
    w_ba = w_ba.at[:, 128:128 + GDN_HEADS].set(wl[:, zc + GDN_HEADS:zc + 2 * GDN_HEADS]).astype(BF16)
    proj, ba = _inproj(xf, norm1_g[0].reshape(1, d), sc1, sh1, w_main, w_ba, seq)

    a_log128 = jnp.zeros((1, 128), F32).at[0, :GDN_HEADS].set(gdn_a_log[0])
    dtb128 = jnp.zeros((1, 128), F32).at[0, :GDN_HEADS].set(gdn_dt_bias[0])
    o_g = _gdn(proj.reshape(batch, seq, -1), ba.reshape(batch, seq, -1), conv_w[0], a_log128, dtb128,
               gdn_norm_g[0].reshape(1, GDN_DV)).reshape(t, GDN_V_W)
    o_r = _ret(proj, posb, inv_freq, ret_norm_g[0].reshape(1, RET_W), ret_norm_b[0].reshape(1, RET_W), batch, seq)

    wo = w_out[0].astype(BF16)
    x1, h2, top_idx, gates = _outproj(o_g, o_r, wo[:GDN_V_W], wo[GDN_V_W:], xf, gt1, norm2_g[0].reshape(1, d),
                                      sc2, sh2, w_router[0].T, b_router[0].reshape(N_EXPERTS, 1), seq)

    src_tok, pos, block_e, nvalid, nused, nv_piece, first_piece = _route(top_idx, MOE_TM)
    x_sorted = _gather(nv_piece, first_piece, nused * (MOE_TM // GATHER_TM), src_tok, h2)
    y_sorted = _moe(block_e, nvalid, nused, x_sorted, w1[0], b1[0], w2[0], b2[0])
    pos_tiles = pos.reshape(TOP_K, t // FINAL_TM, FINAL_TM).transpose(1, 0, 2).reshape(t // FINAL_TM, 1, -1)
    out = _final(pos_tiles, x1, y_sorted, gates.T, gt2, final_norm_g.reshape(1, d), seq)
    return out.reshape(batch, seq, d)
```

```python
import math

import jax
import jax.numpy as jnp
from jax import lax
from jax.experimental import pallas as pl
from jax.experimental.pallas import tpu as pltpu

F32 = jnp.float32
BF16 = jnp.bfloat16
HIGHEST = lax.Precision.HIGHEST

EPS = 1e-6
GDN_HEADS = 8
GDN_DK = 128
GDN_DV = 128
CONV_WIDTH = 4
GDN_CHUNK = 64
GDN_NB = 4
RET_HEADS = 4
RET_DK = 256
RET_DV = 256
RET_CHUNK = 256
ROPE_BASE = 10000.0
N_EXPERTS = 32
TOP_K = 4
SWIGLU_LIMIT = 7.0
SWIGLU_ALPHA = 1.702

GDN_QK_W = GDN_HEADS * GDN_DK
GDN_V_W = GDN_HEADS * GDN_DV
GDN_CONV_CH = 2 * GDN_QK_W + GDN_V_W
RET_W = RET_HEADS * RET_DK

VMEM_LIMIT = 56 * 1024 * 1024
MOE_VMEM_LIMIT = 60 * 1024 * 1024

MOE_TM = 2304
MOE_ROWS = tuple(range(256, MOE_TM + 1, 256))
MOE_TF = 256
GATHER_TM = 768
TOK_PIECE = 1024
FINAL_TM = 256


def _dot(a, b, precision=None):
    return jnp.dot(a, b, preferred_element_type=F32, precision=precision)


def _dot_nt(a, b, precision=None):
    return lax.dot_general(a, b, (((1,), (1,)), ((), ())), preferred_element_type=F32, precision=precision)


def _dot_tn(a, b, precision=None):
    return lax.dot_general(a, b, (((0,), (0,)), ((), ())), preferred_element_type=F32, precision=precision)


def _sigmoid(x):
    return 1.0 / (1.0 + jnp.exp(-x))


def _silu(x):
    return x * _sigmoid(x)


def _cparams(sem, vmem=VMEM_LIMIT):
    return pltpu.CompilerParams(dimension_semantics=sem, vmem_limit_bytes=vmem)


def _ada_kernel(c_ref, w_ref, b_ref, o_ref):
    cond = _silu(c_ref[...])
    o_ref[...] = _dot(cond, w_ref[...], precision=HIGHEST) + b_ref[...]


def _ada(c8, ada_w, ada_b):
    d, n = ada_w.shape
    tn = 1024
    return pl.pallas_call(
        _ada_kernel,
        grid=(n // tn,),
        in_specs=[pl.BlockSpec((8, d), lambda j: (0, 0)),
                  pl.BlockSpec((d, tn), lambda j: (0, j)),
                  pl.BlockSpec((1, tn), lambda j: (0, j))],
        out_specs=pl.BlockSpec((8, tn), lambda j: (0, j)),
        out_shape=jax.ShapeDtypeStruct((8, n), F32),
        compiler_params=_cparams(("arbitrary",)),
        name="ada_mod",
    )(c8, ada_w, ada_b.reshape(1, n))


def _inproj_kernel(x_ref, g_ref, sc_ref, sh_ref, w_ref, wba_ref, proj_ref, ba_ref, h_ref):
    @pl.when(pl.program_id(1) == 0)
    def _():
        x = x_ref[...]
        y = x * lax.rsqrt(jnp.mean(x * x, axis=-1, keepdims=True) + EPS)
        h = (y * g_ref[...]) * (1.0 + sc_ref[0]) + sh_ref[0]
        hb = h.astype(BF16)
        h_ref[...] = hb
        ba_ref[...] = _dot(hb, wba_ref[...])

    proj_ref[...] = _dot(h_ref[...], w_ref[...]).astype(BF16)


def _inproj(xf, g1, sc1, sh1, w_main, w_ba, seq):
    t, d = xf.shape
    n = w_main.shape[1]
    tm, tn = 1024, 2048
    per_b = seq // tm
    return pl.pallas_call(
        _inproj_kernel,
        grid=(t // tm, n // tn),
        in_specs=[pl.BlockSpec((tm, d), lambda i, j: (i, 0)),
                  pl.BlockSpec((1, d), lambda i, j: (0, 0)),
                  pl.BlockSpec((1, 1, d), lambda i, j: (i // per_b, 0, 0)),
                  pl.BlockSpec((1, 1, d), lambda i, j: (i // per_b, 0, 0)),
                  pl.BlockSpec((d, tn), lambda i, j: (0, j)),
                  pl.BlockSpec((d, 256), lambda i, j: (0, 0))],
        out_specs=[pl.BlockSpec((tm, tn), lambda i, j: (i, j)),
                   pl.BlockSpec((tm, 256), lambda i, j: (i, 0))],
        out_shape=[jax.ShapeDtypeStruct((t, n), BF16), jax.ShapeDtypeStruct((t, 256), F32)],
        scratch_shapes=[pltpu.VMEM((tm, d), BF16)],
        compiler_params=_cparams(("arbitrary", "arbitrary")),
        name="norm1_inproj",
    )(xf, g1, sc1, sh1, w_main, w_ba)


def _mm_each(xs, ys):
    return [_dot(x.astype(BF16), y.astype(BF16)) for x, y in zip(xs, ys)]


def _unit_lower_inverses(lms, n):
    r = lax.broadcasted_iota(jnp.int32, (n, n), 0)
    c = lax.broadcasted_iota(jnp.int32, (n, n), 1)
    eye = (r == c).astype(F32)
    diag_blk = (r // 8) == (c // 8)
    d1 = [jnp.where(diag_blk, lm, 0.0) for lm in lms]
    d2 = _mm_each(d1, d1)
    x = [eye - d for d in d1]
    xd = _mm_each(x, d2)
    d4 = _mm_each(d2, d2)
    x = [a + b for a, b in zip(x, xd)]
    xd = _mm_each(x, d4)
    x = [a + b for a, b in zip(x, xd)]
    s = 8
    while s < n:
        off_blk = ((r // (2 * s)) == (c // (2 * s))) & ((r // s) != (c // s))
        off = [jnp.where(off_blk, lm, 0.0) for lm in lms]
        xo = _mm_each(x, off)
        xox = _mm_each(xo, x)
        x = [a - b for a, b in zip(x, xox)]
        s *= 2
    return x


def _gdn_kernel(qkvz_ref, ba_ref, convw_ref, alog_ref, dtb_ref, ng_ref, o_ref, hist_ref, stage_ref, s_ref):
    C = GDN_CHUNK
    nb = qkvz_ref.shape[0]
    items = [(b, h) for b in range(nb) for h in range(GDN_HEADS)]

    @pl.when(pl.program_id(1) == 0)
    def _():
        hist_ref[...] = jnp.zeros_like(hist_ref)
        s_ref[...] = jnp.zeros_like(s_ref)

    r = lax.broadcasted_iota(jnp.int32, (C, C), 0)
    c = lax.broadcasted_iota(jnp.int32, (C, C), 1)
    incl = r >= c
    strict = r > c
    tri = incl.astype(F32)

    beta_all, gstep = [], []
    for b in range(nb):
        u_in = qkvz_ref[b, :, :GDN_CONV_CH].astype(F32)
        stage_ref[b, 0:8, :] = hist_ref[b]
        stage_ref[b, 8:8 + C, :] = u_in
        hist_ref[b] = u_in[C - 8:, :]
        beta_all.append(_sigmoid(ba_ref[b, :, :128]))
        a_in = ba_ref[b, :, 128:] + dtb_ref[...]
        softplus = jnp.maximum(a_in, 0.0) + jnp.log(1.0 + jnp.exp(-jnp.abs(a_in)))
        gstep.append(-jnp.exp(alog_ref[...]) * softplus)
    gcum = [_dot(tri, gs, precision=HIGHEST) for gs in gstep]
    gcum_t = [gc.T for gc in gcum]

    def conv(b, col0):
        acc = None
        for i in range(CONV_WIDTH):
            term = stage_ref[b, 5 + i:5 + i + C, col0:col0 + 128] * convw_ref[i:i + 1, col0:col0 + 128]
            acc = term if acc is None else acc + term
        return _silu(acc)

    qc = [conv(b, h * GDN_DK) for b, h in items]
    kc = [conv(b, GDN_QK_W + h * GDN_DK) for b, h in items]
    vc = [conv(b, 2 * GDN_QK_W + h * GDN_DV) for b, h in items]
    qn = [q * (lax.rsqrt(jnp.sum(q * q, axis=-1, keepdims=True) + EPS) * (GDN_DK ** -0.5)) for q in qc]
    kn = [k * lax.rsqrt(jnp.sum(k * k, axis=-1, keepdims=True) + EPS) for k in kc]
    kb = [k.astype(BF16) for k in kn]
    beta = [beta_all[b][:, h:h + 1] for b, h in items]
    g = [gcum[b][:, h:h + 1] for b, h in items]
    grow = [gcum_t[b][h:h + 1, :] for b, h in items]
    glast = [gcum[b][C - 1:C, h:h + 1] for b, h in items]
    dec_incl = [jnp.exp(jnp.where(incl, gi - gr, -1e30)) for gi, gr in zip(g, grow)]
    kk = [_dot_nt(k, k) for k in kb]
    lms = [bt * kki * jnp.where(strict, di, 0.0) for bt, kki, di in zip(beta, kk, dec_incl)]
    ainv = _unit_lower_inverses(lms, C)
    eg = [jnp.exp(gi) for gi in g]
    rhs = [jnp.concatenate([bt * v, (bt * e) * k], axis=1) for bt, v, e, k in zip(beta, vc, eg, kn)]
    sol = _mm_each(ainv, rhs)
    qk = [_dot_nt(q.astype(BF16), k) for q, k in zip(qn, kb)]
    aqk = [a * di for a, di in zip(qk, dec_incl)]
    q_dec = [q * e for q, e in zip(qn, eg)]
    k_end = [k * jnp.exp(gl - gi) for k, gl, gi in zip(kn, glast, g)]

    state = [s_ref[i] for i in range(len(items))]
    sb = [s.astype(BF16) for s in state]
    ws = [_dot(so[:, GDN_DV:].astype(BF16), s) for so, s in zip(sol, sb)]
    qs = [_dot(q.astype(BF16), s) for q, s in zip(q_dec, sb)]
    delta = [(so[:, :GDN_DV] - w).astype(BF16) for so, w in zip(sol, ws)]
    ad = [_dot(a.astype(BF16), d) for a, d in zip(aqk, delta)]
    kd = [_dot_tn(k.astype(BF16), d) for k, d in zip(k_end, delta)]
    for i, (b, h) in enumerate(items):
        s_ref[i] = jnp.exp(glast[i]) * state[i] + kd[i]
        out = qs[i] + ad[i]
        z = qkvz_ref[b, :, GDN_CONV_CH + h * GDN_DV:GDN_CONV_CH + (h + 1) * GDN_DV].astype(F32)
        o = out * lax.rsqrt(jnp.mean(out * out, axis=-1, keepdims=True) + EPS) * ng_ref[...] * _silu(z)
        o_ref[b, :, h * GDN_DV:(h + 1) * GDN_DV] = o.astype(BF16)


def _gdn(proj3, ba3, conv_w, a_log128, dtb128, norm_g):
    batch, seq, _ = proj3.shape
    C = GDN_CHUNK
    nb = GDN_NB if batch % GDN_NB == 0 else 1
    return pl.pallas_call(
        _gdn_kernel,
        grid=(batch // nb, seq // C),
        in_specs=[pl.BlockSpec((nb, C, 4096), lambda b, s: (b, s, 0)),
                  pl.BlockSpec((nb, C, 256), lambda b, s: (b, s, 0)),
                  pl.BlockSpec((CONV_WIDTH, GDN_CONV_CH), lambda b, s: (0, 0)),
                  pl.BlockSpec((1, 128), lambda b, s: (0, 0)),
                  pl.BlockSpec((1, 128), lambda b, s: (0, 0)),
                  pl.BlockSpec((1, 128), lambda b, s: (0, 0))],
        out_specs=pl.BlockSpec((nb, C, GDN_V_W), lambda b, s: (b, s, 0)),
        out_shape=jax.ShapeDtypeStruct((batch, seq, GDN_V_W), BF16),
        scratch_shapes=[pltpu.VMEM((nb, 8, GDN_CONV_CH), F32),
                        pltpu.VMEM((nb, C + 8, GDN_CONV_CH), F32),
                        pltpu.VMEM((nb * GDN_HEADS, GDN_DK, GDN_DV), F32)],
        compiler_params=_cparams(("arbitrary", "arbitrary")),
        name="gdn_heads",
    )(proj3, ba3, conv_w, a_log128, dtb128, norm_g)


def _ret_kernel(p_ref, pos_ref, invf_ref, ng_ref, nb_ref, o_ref, s_ref):
    C = RET_CHUNK

    @pl.when(pl.program_id(1) == 0)
    def _():
        s_ref[...] = jnp.zeros_like(s_ref)

    ang = pos_ref[...] * invf_ref[...]
    cs = jnp.cos(ang)
    sn = jnp.sin(ang)
    r = lax.broadcasted_iota(jnp.int32, (C, C), 0)
    c = lax.broadcasted_iota(jnp.int32, (C, C), 1)
    incl = r >= c
    dpos = (r - c).astype(F32)
    pcol = lax.broadcasted_iota(jnp.int32, (C, 1), 0).astype(F32)
    half = RET_DK // 2

    for h in range(RET_HEADS):
        lg = math.log1p(-(2.0 ** (-5.0 - h)))
        q0 = h * RET_DK
        k0 = RET_W + h * RET_DK
        v0 = 2 * RET_W + h * RET_DV
        g0 = 3 * RET_W + h * RET_DV
        q1 = p_ref[:, q0:q0 + half].astype(F32)
        q2 = p_ref[:, q0 + half:q0 + RET_DK].astype(F32)
        k1 = p_ref[:, k0:k0 + half].astype(F32)
        k2 = p_ref[:, k0 + half:k0 + RET_DK].astype(F32)
        q = jnp.concatenate([q1 * cs - q2 * sn, q2 * cs + q1 * sn], axis=1)
        k = jnp.concatenate([k1 * cs - k2 * sn, k2 * cs + k1 * sn], axis=1) * (RET_DK ** -0.5)
        v = p_ref[:, v0:v0 + RET_DV]
        dmat = jnp.exp(jnp.where(incl, dpos * lg, -1e30))
        scores = _dot_nt(q.astype(BF16), k.astype(BF16)) * dmat
        inner = _dot(scores.astype(BF16), v)
        q_dec = q * jnp.exp((pcol + 1.0) * lg)
        k_dec = k * jnp.exp((C - 1.0 - pcol) * lg)
        state = s_ref[h]
        out = inner + _dot(q_dec.astype(BF16), state.astype(BF16))
        s_ref[h] = math.exp(C * lg) * state + _dot_tn(k_dec.astype(BF16), v)
        mu = jnp.mean(out, axis=-1, keepdims=True)
        cen = out - mu
        var = jnp.mean(cen * cen, axis=-1, keepdims=True)
        on = cen * lax.rsqrt(var + EPS)
        gate = p_ref[:, g0:g0 + RET_DV].astype(F32)
        o = (on * ng_ref[:, h * RET_DV:(h + 1) * RET_DV] + nb_ref[:, h * RET_DV:(h + 1) * RET_DV]) * _silu(gate)
        o_ref[:, h * RET_DV:(h + 1) * RET_DV] = o.astype(BF16)


def _ret(proj, posb, inv_freq, norm_g, norm_b, batch, seq):
    t = proj.shape[0]
    C = RET_CHUNK
    nc = seq // C
    return pl.pallas_call(
        _ret_kernel,
        grid=(batch, nc),
        in_specs=[pl.BlockSpec((C, 4096), lambda b, s: (b * nc + s, 1)),
                  pl.BlockSpec((C, 128), lambda b, s: (b * nc + s, 0)),
                  pl.BlockSpec((1, 128), lambda b, s: (0, 0)),
                  pl.BlockSpec((1, RET_W), lambda b, s: (0, 0)),
                  pl.BlockSpec((1, RET_W), lambda b, s: (0, 0))],
        out_specs=pl.BlockSpec((C, RET_W), lambda b, s: (b * nc + s, 0)),
        out_shape=jax.ShapeDtypeStruct((t, RET_W), BF16),
        scratch_shapes=[pltpu.VMEM((RET_HEADS, RET_DK, RET_DV), F32)],
        compiler_params=_cparams(("arbitrary", "arbitrary")),
        name="ret_heads",
    )(proj, posb, inv_freq, norm_g, norm_b)


def _outproj_kernel(og_ref, or_ref, wt_ref, wb_ref, x_ref, gt_ref, g2_ref, sc_ref, sh_ref, wr_ref, br_ref,
                    x1_ref, h2_ref, idx_ref, gate_ref):
    mix = _dot(og_ref[...], wt_ref[...]) + _dot(or_ref[...], wb_ref[...])
    x1 = x_ref[...] + gt_ref[0] * mix
    x1_ref[...] = x1
    y = x1 * lax.rsqrt(jnp.mean(x1 * x1, axis=-1, keepdims=True) + EPS)
    h2 = (y * g2_ref[...]) * (1.0 + sc_ref[0]) + sh_ref[0]
    h2_ref[...] = h2
    logits = _dot_nt(wr_ref[...], h2, precision=HIGHEST) + br_ref[...]
    eidx = lax.broadcasted_iota(jnp.int32, logits.shape, 0)
    vals = logits
    tops, idxs = [], []
    for _ in range(TOP_K):
        m = jnp.max(vals, axis=0, keepdims=True)
        sel = jnp.min(jnp.where(vals == m, eidx, N_EXPERTS), axis=0, keepdims=True)
        tops.append(m)
        idxs.append(sel)
        vals = jnp.where(eidx == sel, -jnp.inf, vals)
    exps = [jnp.exp(v - tops[0]) for v in tops]
    denom = exps[0] + exps[1] + exps[2] + exps[3]
    idx_ref[...] = jnp.concatenate(idxs, axis=0)
    gate_ref[...] = jnp.concatenate(exps, axis=0) / denom


def _outproj(o_g, o_r, w_top, w_bot, xf, gt1, g2, sc2, sh2, wr_t, br, seq):
    t, d = xf.shape
    tm = 512
    per_b = seq // tm
    row = lambda i: (i, 0)
    full = lambda i: (0, 0)
    perb = lambda i: (i // per_b, 0, 0)
    return pl.pallas_call(
        _outproj_kernel,
        grid=(t // tm,),
        in_specs=[pl.BlockSpec((tm, GDN_V_W), row),
                  pl.BlockSpec((tm, RET_W), row),
                  pl.BlockSpec((GDN_V_W, d), full),
                  pl.BlockSpec((RET_W, d), full),
                  pl.BlockSpec((tm, d), row),
                  pl.BlockSpec((1, 1, d), perb),
                  pl.BlockSpec((1, d), full),
                  pl.BlockSpec((1, 1, d), perb),
                  pl.BlockSpec((1, 1, d), perb),
                  pl.BlockSpec((N_EXPERTS, d), full),
                  pl.BlockSpec((N_EXPERTS, 1), full)],
        out_specs=[pl.BlockSpec((tm, d), row),
                   pl.BlockSpec((tm, d), row),
                   pl.BlockSpec((TOP_K, tm), lambda i: (0, i)),
                   pl.BlockSpec((TOP_K, tm), lambda i: (0, i))],
        out_shape=[jax.ShapeDtypeStruct((t, d), F32), jax.ShapeDtypeStruct((t, d), F32),
                   jax.ShapeDtypeStruct((TOP_K, t), jnp.int32), jax.ShapeDtypeStruct((TOP_K, t), F32)],
        compiler_params=_cparams(("arbitrary",)),
        name="outproj_norm2_router",
    )(o_g, o_r, w_top, w_bot, xf, gt1, g2, sc2, sh2, wr_t, br)


ROW_GROUP = 8


def _start_copy(copy, parity):
    copy.start(priority=parity)


def _wait_copy(copy, parity):
    copy.wait()


def _rows_to_vmem(index_of, n_rows, rows_hbm, dst_ref, sem, act):
    def group(g, carry):
        idx = [index_of(g * ROW_GROUP + u) for u in range(ROW_GROUP)]
        for u, s in enumerate(idx):
            act(pltpu.make_async_copy(rows_hbm.at[pl.ds(s, 1)], dst_ref.at[g, pl.ds(u, 1)], sem), u % 2)
        return carry

    lax.fori_loop(0, (n_rows + ROW_GROUP - 1) // ROW_GROUP, group, 0)


def _gather_kernel(nv_ref, first_ref, nu_ref, tok_ref, tok_next_ref, h_hbm, o_ref, stage_ref, sem):
    g = pl.program_id(0)
    slot = g % 2
    gm = o_ref.shape[0]

    def fetch(window_ref, piece, half, act):
        first = first_ref[piece]
        off = first - jnp.minimum(first // TOK_PIECE, nu_ref[1] - 1) * TOK_PIECE
        _rows_to_vmem(lambda u: window_ref[0, 0, off + u], nv_ref[piece], h_hbm, stage_ref.at[half], sem.at[half], act)

    @pl.when(g == 0)
    def _():
        stage_ref[...] = jnp.zeros_like(stage_ref)
        fetch(tok_ref, 0, 0, _start_copy)

    @pl.when(g + 1 < pl.num_programs(0))
    def _():
        fetch(tok_next_ref, g + 1, 1 - slot, _start_copy)

    fetch(tok_ref, g, slot, _wait_copy)

    @pl.when(g < nu_ref[0])
    def _():
        rows = stage_ref[slot].reshape(gm, o_ref.shape[1])
        real = lax.broadcasted_iota(jnp.int32, (gm, 1), 0) < nv_ref[g]
        o_ref[...] = jnp.where(real, rows, 0.0).astype(o_ref.dtype)


def _gather(nv_piece, first_piece, n_used_pieces, src_tok, h2):
    n = src_tok.shape[0]
    n_tok_pieces = n // TOK_PIECE
    pieces = nv_piece.shape[0]
    dh = h2.shape[1]
    gm = GATHER_TM
    tok = src_tok.reshape(n_tok_pieces, TOK_PIECE)
    tok_after = jnp.concatenate([tok[1:], jnp.zeros((1, TOK_PIECE), jnp.int32)], axis=0)
    windows = jnp.concatenate([tok, tok_after], axis=1).reshape(n_tok_pieces, 1, 2 * TOK_PIECE)
    counts = jnp.concatenate([n_used_pieces, jnp.full((1,), n_tok_pieces, jnp.int32)])

    def window(g, fr):
        return jnp.minimum(fr[g] // TOK_PIECE, n_tok_pieces - 1)

    return pl.pallas_call(
        _gather_kernel,
        grid_spec=pltpu.PrefetchScalarGridSpec(
            num_scalar_prefetch=3,
            grid=(pieces,),
            in_specs=[pl.BlockSpec((1, 1, 2 * TOK_PIECE), lambda g, nv, fr, nu: (window(g, fr), 0, 0),
                                   memory_space=pltpu.SMEM),
                      pl.BlockSpec((1, 1, 2 * TOK_PIECE),
                                   lambda g, nv, fr, nu: (window(jnp.minimum(g + 1, pieces - 1), fr), 0, 0),
                                   memory_space=pltpu.SMEM),
                      pl.BlockSpec(memory_space=pl.ANY)],
            out_specs=pl.BlockSpec((gm, dh), lambda g, nv, fr, nu: (jnp.minimum(g, nu[0] - 1), 0)),
            scratch_shapes=[pltpu.VMEM((2, gm // ROW_GROUP, ROW_GROUP, dh), h2.dtype),
                            pltpu.SemaphoreType.DMA((2,))],
        ),
        out_shape=jax.ShapeDtypeStruct((pieces * gm, dh), BF16),
        compiler_params=_cparams(("arbitrary",)),
        name="row_gather",
    )(nv_piece, first_piece, counts, windows, windows, h2)


def _moe_kernel(be_ref, nv_ref, nu_ref, x_ref, w1g_ref, w1l_ref, b1g_ref, b1l_ref, w2_ref, b2_ref,
                o_ref, act_ref):
    i = pl.program_id(0)
    j = pl.program_id(1)
    nj = act_ref.shape[0]
    tm = x_ref.shape[0]
    nv = nv_ref[i]
    used = i < nu_ref[0]

    def up(m):
        xs = x_ref[0:m, :]
        hg = _dot(xs, w1g_ref[0].astype(BF16)) + b1g_ref[0]
        hl = _dot(xs, w1l_ref[0].astype(BF16)) + b1l_ref[0]
        hg = jnp.minimum(hg, SWIGLU_LIMIT)
        hl = jnp.clip(hl, -SWIGLU_LIMIT, SWIGLU_LIMIT)
        return (hg * _sigmoid(SWIGLU_ALPHA * hg) * (hl + 1.0)).astype(BF16)

    def down(m):
        y = b2_ref[0] + _dot(act_ref[0, 0:m, :], w2_ref[0, 0:MOE_TF, :].astype(BF16))
        for jj in range(1, nj):
            y = y + _dot(act_ref[jj, 0:m, :], w2_ref[0, jj * MOE_TF:(jj + 1) * MOE_TF, :].astype(BF16))
        return y

    lo = 0
    for m in MOE_ROWS:
        fits = jnp.logical_and(used, jnp.logical_and(nv > lo, nv <= m))

        @pl.when(jnp.logical_and(fits, j < nj))
        def _():
            act_ref[j, 0:m, :] = up(m)

        @pl.when(jnp.logical_and(fits, j >= nj))
        def _():
            o_ref[0:m, :] = down(m)
            if m < tm:
                o_ref[m:tm, :] = jnp.zeros((tm - m, o_ref.shape[1]), o_ref.dtype)

        lo = m


def _moe(block_e, nvalid, nused, x_sorted, w1, b1, w2, b2):
    r = x_sorted.shape[0]
    e, d, f2 = w1.shape
    f = f2 // 2
    tm, tf = MOE_TM, MOE_TF
    assert MOE_ROWS[-1] == tm
    nblk = r // tm
    nj = f // tf
    assert d // tf == nj

    def live(i, nu):
        return jnp.minimum(i, nu[0] - 1)

    def c1(i, j, nu):
        return jnp.where(i < nu[0], jnp.minimum(j, nj - 1), nj - 1)

    def c2(i, j, nu):
        return jnp.where(i < nu[0], jnp.maximum(j - nj, 0), nj - 1)

    grid_spec = pltpu.PrefetchScalarGridSpec(
        num_scalar_prefetch=3,
        grid=(nblk, 2 * nj),
        in_specs=[
            pl.BlockSpec((tm, d), lambda i, j, be, nv, nu: (live(i, nu), 0)),
            pl.BlockSpec((1, d, tf), lambda i, j, be, nv, nu: (be[live(i, nu)], 0, c1(i, j, nu))),
            pl.BlockSpec((1, d, tf), lambda i, j, be, nv, nu: (be[live(i, nu)], 0, nj + c1(i, j, nu))),
            pl.BlockSpec((1, 1, tf), lambda i, j, be, nv, nu: (be[live(i, nu)], 0, c1(i, j, nu))),
            pl.BlockSpec((1, 1, tf), lambda i, j, be, nv, nu: (be[live(i, nu)], 0, nj + c1(i, j, nu))),
            pl.BlockSpec((1, f, tf), lambda i, j, be, nv, nu: (be[live(i, nu)], 0, c2(i, j, nu))),
            pl.BlockSpec((1, 1, tf), lambda i, j, be, nv, nu: (be[live(i, nu)], 0, c2(i, j, nu))),
        ],
        out_specs=pl.BlockSpec((tm, tf), lambda i, j, be, nv, nu: (live(i, nu), c2(i, j, nu))),
        scratch_shapes=[pltpu.VMEM((nj, tm, tf), BF16)],
    )
    return pl.pallas_call(
        _moe_kernel,
        grid_spec=grid_spec,
        out_shape=jax.ShapeDtypeStruct((r, d), F32),
        compiler_params=_cparams(("arbitrary", "arbitrary"), vmem=MOE_VMEM_LIMIT),
        name="moe_experts",
    )(block_e, nvalid, nused, x_sorted, w1, w1, b1.reshape(e, 1, f2), b1.reshape(e, 1, f2), w2,
      b2.reshape(e, 1, d))


def _final_kernel(pos_ref, pos_next_ref, x1_ref, gk_ref, gt_ref, g_ref, y_hbm, o_ref, buf_ref, sem):
    i = pl.program_id(0)
    slot = i % 2
    tm = x1_ref.shape[0]
    n_rows = TOP_K * tm

    this_tile = lambda u: pos_ref[0, 0, u]
    next_tile = lambda u: pos_next_ref[0, 0, u]

    @pl.when(i == 0)
    def _():
        _rows_to_vmem(this_tile, n_rows, y_hbm, buf_ref.at[0], sem.at[0], _start_copy)

    @pl.when(i + 1 < pl.num_programs(0))
    def _():
        _rows_to_vmem(next_tile, n_rows, y_hbm, buf_ref.at[1 - slot], sem.at[1 - slot], _start_copy)

    _rows_to_vmem(this_tile, n_rows, y_hbm, buf_ref.at[slot], sem.at[slot], _wait_copy)

    per_slot = tm // ROW_GROUP
    moe = None
    for k in range(TOP_K):
        yk = buf_ref[slot, k * per_slot:(k + 1) * per_slot].reshape(tm, buf_ref.shape[3])
        term = gk_ref[:, k:k + 1] * yk
        moe = term if moe is None else moe + term
    x2 = x1_ref[...] + gt_ref[0] * moe
    o_ref[...] = x2 * lax.rsqrt(jnp.mean(x2 * x2, axis=-1, keepdims=True) + EPS) * g_ref[...]


def _final(pos_tiles, x1, y_sorted, gates_tk, gt2, final_g, seq):
    t, d = x1.shape
    nt, _, n_rows = pos_tiles.shape
    tm = n_rows // TOP_K
    per_b = seq // tm
    return pl.pallas_call(
        _final_kernel,
        grid=(nt,),
        in_specs=[pl.BlockSpec((1, 1, n_rows), lambda i: (i, 0, 0), memory_space=pltpu.SMEM),
                  pl.BlockSpec((1, 1, n_rows), lambda i: (jnp.minimum(i + 1, nt - 1), 0, 0),
                               memory_space=pltpu.SMEM),
                  pl.BlockSpec((tm, d), lambda i: (i, 0)),
                  pl.BlockSpec((tm, TOP_K), lambda i: (i, 0)),
                  pl.BlockSpec((1, 1, d), lambda i: (i // per_b, 0, 0)),
                  pl.BlockSpec((1, d), lambda i: (0, 0)),
                  pl.BlockSpec(memory_space=pl.ANY)],
        out_specs=pl.BlockSpec((tm, d), lambda i: (i, 0)),
        out_shape=jax.ShapeDtypeStruct((t, d), F32),
        scratch_shapes=[pltpu.VMEM((2, n_rows // ROW_GROUP, ROW_GROUP, y_sorted.shape[1]), y_sorted.dtype),
                        pltpu.SemaphoreType.DMA((2,))],
        compiler_params=_cparams(("arbitrary",)),
        name="combine_final_norm",
    )(pos_tiles, pos_tiles, x1, gates_tk, gt2, final_g, y_sorted)


def _route(top_idx, tm):
    k, t = top_idx.shape
    n = t * k
    flat_e = top_idx.T.reshape(n)
    se, order = lax.sort((flat_e, jnp.arange(n, dtype=jnp.int32)), num_keys=1, is_stable=True)
    experts = jnp.arange(N_EXPERTS, dtype=jnp.int32)
    onehot = se[:, None] == experts[None, :]
    counts = jnp.sum(onehot.astype(jnp.int32), axis=0)
    starts = jnp.cumsum(counts) - counts
    nb_e = (counts + tm - 1) // tm
    per_blk = (counts + jnp.maximum(nb_e, 1) * 8 - 1) // (jnp.maximum(nb_e, 1) * 8) * 8
    blk_ends = jnp.cumsum(nb_e)
    blk_starts = blk_ends - nb_e

    def of_expert(table, hot):
        return jnp.sum(jnp.where(hot, table[None, :], 0), axis=1)

    rank = jnp.arange(n, dtype=jnp.int32) - of_expert(starts, onehot)
    per_s = jnp.maximum(of_expert(per_blk, onehot), 1)
    jblk = rank // per_s
    dst = (of_expert(blk_starts, onehot) + jblk) * tm + rank - jblk * per_s
    src_tok = order // k
    slot_major = (order % k) * t + order // k
    _, pos = lax.sort((slot_major, dst), num_keys=1)
    nblk = n // tm + N_EXPERTS
    blk = jnp.arange(nblk, dtype=jnp.int32)
    block_e = jnp.minimum(jnp.sum((blk_ends[None, :] <= blk[:, None]).astype(jnp.int32), axis=1), N_EXPERTS - 1)
    eh = block_e[:, None] == experts[None, :]
    per_b = of_expert(per_blk, eh)
    done = (blk - of_expert(blk_starts, eh)) * per_b
    nvalid = jnp.clip(of_expert(counts, eh) - done, 0, per_b).astype(jnp.int32)
    first = of_expert(starts, eh) + done
    nused = blk_ends[-1].astype(jnp.int32).reshape(1)
    per = tm // GATHER_TM
    sub = jnp.arange(per, dtype=jnp.int32)[None, :] * GATHER_TM
    nv_piece = jnp.clip(nvalid[:, None] - sub, 0, GATHER_TM).astype(jnp.int32).reshape(nblk * per)
    first_piece = jnp.clip(first[:, None] + sub, 0, n).astype(jnp.int32).reshape(nblk * per)
    return src_tok, pos, block_e.astype(jnp.int32), nvalid, nused, nv_piece, first_piece


def kernel(x, c, positions, ada_w, ada_b, norm1_g, w_in, conv_w, gdn_a_log, gdn_dt_bias, gdn_norm_g, ret_norm_g,
           ret_norm_b, w_out, norm2_g, w_router, b_router, w1, b1, w2, b2, final_norm_g):
    batch, seq, d = x.shape
    t = batch * seq
    assert ada_w.shape[0] == 1, "the block is specified at depth 1"
    xf = x.reshape(t, d)
    c8 = jnp.zeros((8, d), F32).at[:batch].set(c)
    half = RET_DK // 2
    inv_freq = jnp.power(ROPE_BASE, -jnp.linspace(0.0, 1.0, half, dtype=F32)).reshape(1, half)
    posb = jnp.broadcast_to(positions.astype(F32).reshape(t, 1), (t, half))

    mod = _ada(c8, ada_w[0], ada_b[0])[:batch]
    sh1, sc1, gt1, sh2, sc2, gt2 = [m.reshape(batch, 1, d) for m in jnp.split(mod, 6, axis=-1)]

    zc = GDN_CONV_CH + GDN_V_W
    wl = w_in[0]
    w_main = jnp.concatenate([wl[:, :zc], wl[:, zc + 2 * GDN_HEADS:]], axis=1).astype(BF16)
    w_ba = jnp.zeros((d, 256), F32)
    w_ba = w_ba.at[:, :GDN_HEADS].set(wl[:, zc:zc + GDN_HEADS])
    w_ba = w_ba.at[:, 128:128 + GDN_HEADS].set(wl[:, zc + GDN_HEADS:zc + 2 * GDN_HEADS]).astype(BF16)
    proj, ba = _inproj(xf, norm1_g[0].reshape(1, d), sc1, sh1, w_main, w_ba, seq)

    a_log128 = jnp.zeros((1, 128), F32).at[0, :GDN_HEADS].set(gdn_a_log[0])
    dtb128 = jnp.zeros((1, 128), F32).at[0, :GDN_HEADS].set(gdn_dt_bias[0])
    o_g = _gdn(proj.reshape(batch, seq, -1), ba.reshape(batch, seq, -1), conv_w[0], a_log128, dtb128,
               gdn_norm_g[0].reshape(1, GDN_DV)).reshape(t, GDN_V_W)
    o_r = _ret(proj, posb, inv_freq, ret_norm_g[0].reshape(1, RET_W), ret_norm_b[0].reshape(1, RET_W), batch, seq)

    wo = w_out[0].astype(BF16)
    x1, h2, top_idx, gates = _outproj(o_g, o_r, wo[:GDN_V_W], wo[GDN_V_W:], xf, gt1, norm2_g[0].reshape(1, d),
                                      sc2, sh2, w_router[0].T, b_router[0].reshape(N_EXPERTS, 1), seq)

    src_tok, pos, block_e, nvalid, nused, nv_piece, first_piece = _route(top_idx, MOE_TM)
    x_sorted = _gather(nv_piece, first_piece, nused * (MOE_TM // GATHER_TM), src_tok, h2)
    y_sorted = _moe(block_e, nvalid, nused, x_sorted, w1[0], b1[0], w2[0], b2[0])
    pos_tiles = pos.reshape(TOP_K, t // FINAL_TM, FINAL_TM).transpose(1, 0, 2).reshape(t // FINAL_TM, 1, -1)
    out = _final(pos_tiles, x1, y_sorted, gates.T, gt2, final_norm_g.reshape(1, d), seq)
    return out.reshape(batch, seq, d)
```
